```python
import math
import jax, jax.numpy as jnp
from jax import lax
import numpy as np

D_MODEL = 1024
BATCH = 8
SEQ = 8192
DEPTH = 1
DEC_BATCH = 128
DEC_SEQ = 1
PAST_LEN = 8192
PAGE_SIZE = 128

D_MIX = D_MODEL
D_MIX_A = D_MIX // 2
D_MIX_B = D_MIX - D_MIX_A
DIFF_HEAD_DIM = 64
DIFF_HEADS = D_MIX_A // (2 * DIFF_HEAD_DIM)
MOBA_HEAD_DIM = 64
MOBA_HEADS = D_MIX_B // MOBA_HEAD_DIM
MOBA_BLOCK = 256
MOBA_TOPK = 3
MOBA_Q_CHUNK = 32
DIFF_Q_BLOCK = 128
ROPE_THETA = 500000.0
ROT_FRAC = 4
D_IN = 3 * D_MIX_A + 3 * D_MIX_B
IN_SPLITS = [D_MIX_A, 2 * D_MIX_A, 3 * D_MIX_A, 3 * D_MIX_A + D_MIX_B, 3 * D_MIX_A + 2 * D_MIX_B]
N_EXPERTS = 64
TOP_K = 8
N_GROUPS = 8
TOPK_GROUPS = 4
D_EXPERT = 256
D_SHARED = D_EXPERT
ROUTED_SCALE = 2.5
MOE_BLOCK = 256
EPS = 1e-6

kernel_name = 'hybrid_diffattn_moba_moe_step'


def rmsnorm(x, g):
    xf = x.astype(jnp.float32)
    inv = lax.rsqrt(jnp.mean(xf * xf, axis=-1, keepdims=True) + EPS)
    return (xf * inv * g.astype(jnp.float32)).astype(x.dtype)


def rope_partial(x, pos):
    rot = x.shape[-1] // ROT_FRAC
    half = rot // 2
    inv_freq = ROPE_THETA ** (-2.0 * jnp.arange(half, dtype=jnp.float32) / rot)
    ang = pos.astype(jnp.float32)[:, None] * inv_freq[None, :]
    shape = (1, pos.shape[0]) + (1,) * (x.ndim - 3) + (half,)
    cos = jnp.cos(ang).reshape(shape)
    sin = jnp.sin(ang).reshape(shape)
    xr = x[..., :rot].astype(jnp.float32)
    x1, x2 = xr[..., :half], xr[..., half:]
    rotated = jnp.concatenate([x1 * cos - x2 * sin, x2 * cos + x1 * sin], axis=-1).astype(x.dtype)
    return jnp.concatenate([rotated, x[..., rot:]], axis=-1)


def swiglu(x, wg, wu, wd):
    return (jax.nn.silu(x @ wg) * (x @ wu)) @ wd


def project_qkv(h, pos, w_in):
    b, l, _ = h.shape
    proj = jnp.einsum('bld,de->ble', h, w_in)
    qd, kd, vd, qm, km, vm = jnp.split(proj, IN_SPLITS, axis=-1)
    qd = rope_partial(qd.reshape(b, l, DIFF_HEADS, 2, DIFF_HEAD_DIM), pos)
    kd = rope_partial(kd.reshape(b, l, DIFF_HEADS, 2, DIFF_HEAD_DIM), pos)
    vd = vd.reshape(b, l, DIFF_HEADS, 2 * DIFF_HEAD_DIM)
    qm = rope_partial(qm.reshape(b, l, MOBA_HEADS, MOBA_HEAD_DIM), pos)
    km = rope_partial(km.reshape(b, l, MOBA_HEADS, MOBA_HEAD_DIM), pos)
    vm = vm.reshape(b, l, MOBA_HEADS, MOBA_HEAD_DIM)
    return qd, kd, vd, qm, km, vm


def diff_core(q, k, v, q_pos, lam, g_sub, lam_init):
    t = k.shape[1]
    s = jnp.einsum('bqhmd,bthmd->bhmqt', q, k).astype(jnp.float32) * (DIFF_HEAD_DIM ** -0.5)
    causal = jnp.arange(t)[None, :] <= q_pos[:, None]
    p = jax.nn.softmax(jnp.where(causal, s, -jnp.inf), axis=-1)
    a = (p[:, :, 0] - lam * p[:, :, 1]).astype(v.dtype)
    o = jnp.einsum('bhqt,bthe->bqhe', a, v)
    return rmsnorm(o, g_sub) * (1.0 - lam_init)


def query_sweep(fn, q, block):
    b, s = q.shape[:2]
    n = s // block
    qb = jnp.moveaxis(q.reshape((b, n, block) + q.shape[2:]), 1, 0)
    pb = jnp.arange(s, dtype=jnp.int32).reshape(n, block)
    o = lax.map(lambda qp: fn(qp[0], qp[1]), (qb, pb))
    return jnp.moveaxis(o, 0, 1).reshape((b, s) + o.shape[3:])


def moba_blocks(k, v):
    b, t = k.shape[:2]
    nb = -(-t // MOBA_BLOCK)
    pad = nb * MOBA_BLOCK - t

    def to_blocks(a):
        a = jnp.pad(a, ((0, 0), (0, pad), (0, 0), (0, 0)))
        return jnp.transpose(a.reshape(b, nb, MOBA_BLOCK, MOBA_HEADS, MOBA_HEAD_DIM), (0, 3, 1, 2, 4))

    kb, vb = to_blocks(k), to_blocks(v)
    k_mean = jnp.mean(kb.astype(jnp.float32), axis=3)
    return kb, vb, k_mean


def moba_core(q, q_pos, kb, vb, k_mean):
    b, nq = q.shape[:2]
    nb = kb.shape[2]
    k_eff = min(MOBA_TOPK, nb)
    q_blk = q_pos // MOBA_BLOCK
    gate = jnp.einsum('bqhd,bhnd->bhqn', q.astype(jnp.float32), k_mean)
    fully_past = jnp.arange(nb)[None, :] < q_blk[:, None]
    _, top = lax.top_k(jnp.where(fully_past, gate, -jnp.inf), k_eff)
    own = jnp.broadcast_to(q_blk[None, None, :, None], (b, MOBA_HEADS, nq, 1))
    sel = jnp.concatenate([top.astype(jnp.int32), own.astype(jnp.int32)], axis=-1)
    sel_valid = jnp.concatenate([jnp.arange(k_eff)[None, :] < q_blk[:, None], jnp.ones((nq, 1), bool)], axis=-1)
    bi = jnp.arange(b)[:, None, None, None]
    hi = jnp.arange(MOBA_HEADS)[None, :, None, None]
    k_sel = kb[bi, hi, sel]
    v_sel = vb[bi, hi, sel]
    s = jnp.einsum('bqhd,bhqnld->bhqnl', q, k_sel).astype(jnp.float32) * (MOBA_HEAD_DIM ** -0.5)
    key_pos = sel[..., None] * MOBA_BLOCK + jnp.arange(MOBA_BLOCK)
    mask = sel_valid[None, None, :, :, None] & (key_pos <= q_pos[None, None, :, None, None])
    s = jnp.where(mask, s, -jnp.inf)
    n = k_eff + 1
    p = jax.nn.softmax(s.reshape(b, MOBA_HEADS, nq, n * MOBA_BLOCK), axis=-1).reshape(s.shape).astype(v_sel.dtype)
    return jnp.einsum('bhqnl,bhqnld->bqhd', p, v_sel)


def token_mixers(h, pos, past, lam_init, w_in, w_out, lambda_q1, lambda_k1, lambda_q2, lambda_k2, g_subln):
    b, l, _ = h.shape
    qd, kd, vd, qm, km, vm = project_qkv(h, pos, w_in)
    f32 = jnp.float32
    lam = (jnp.exp(jnp.dot(lambda_q1.astype(f32), lambda_k1.astype(f32)))
           - jnp.exp(jnp.dot(lambda_q2.astype(f32), lambda_k2.astype(f32))) + lam_init)
    if past is None:
        od = query_sweep(lambda q, p: diff_core(q, kd, vd, p, lam, g_subln, lam_init), qd, DIFF_Q_BLOCK)
        kb, vb, k_mean = moba_blocks(km, vm)
        om = query_sweep(lambda q, p: moba_core(q, p, kb, vb, k_mean), qm, MOBA_Q_CHUNK)
    else:
        kd_past, vd_past, km_past, vm_past = past
        n_past = kd_past.shape[1]
        kd_all = jnp.concatenate([kd_past.reshape(b, n_past, DIFF_HEADS, 2, DIFF_HEAD_DIM), kd], axis=1)
        vd_all = jnp.concatenate([vd_past, vd], axis=1)
        od = diff_core(qd, kd_all, vd_all, pos, lam, g_subln, lam_init)
        kb, vb, k_mean = moba_blocks(jnp.concatenate([km_past, km], axis=1), jnp.concatenate([vm_past, vm], axis=1))
        om = moba_core(qm, pos, kb, vb, k_mean)
    o = jnp.concatenate([od.reshape(b, l, D_MIX_A), om.reshape(b, l, D_MIX_B)], axis=-1)
    y = jnp.einsum('ble,ed->bld', o, w_out)
    rows = (kd.reshape(b, l, DIFF_HEADS, 2 * DIFF_HEAD_DIM), vd, km, vm)
    return y, rows


def moe_ffn(h, w_router, b_router, w_exp_gate, w_exp_up, w_exp_down, w_sh_gate, w_sh_up, w_sh_down):
    n_tok = h.shape[0]
    scores = jax.nn.sigmoid(h.astype(jnp.float32) @ w_router.astype(jnp.float32))
    biased = scores + b_router.astype(jnp.float32)
    grp = biased.reshape(n_tok, N_GROUPS, N_EXPERTS // N_GROUPS)
    grp_score = jnp.sum(lax.top_k(grp, 2)[0], axis=-1)
    _, grp_idx = lax.top_k(grp_score, TOPK_GROUPS)
    grp_mask = jnp.any(grp_idx[:, :, None] == jnp.arange(N_GROUPS)[None, None, :], axis=1)
    exp_mask = jnp.repeat(grp_mask, N_EXPERTS // N_GROUPS, axis=1)
    _, idx = lax.top_k(jnp.where(exp_mask, biased, -jnp.inf), TOP_K)
    wts = jnp.take_along_axis(scores, idx, axis=1)
    wts = wts / jnp.sum(wts, axis=-1, keepdims=True) * ROUTED_SCALE
    n_assign = n_tok * TOP_K
    flat_e = idx.reshape(n_assign).astype(jnp.int32)
    flat_tok = jnp.repeat(jnp.arange(n_tok, dtype=jnp.int32), TOP_K)
    flat_w = wts.reshape(n_assign)
    order = jnp.argsort(flat_e)
    e_sorted = flat_e[order]
    counts = jnp.zeros((N_EXPERTS,), jnp.int32).at[flat_e].add(1)
    padded = (counts + MOE_BLOCK - 1) // MOE_BLOCK * MOE_BLOCK
    pad_end = jnp.cumsum(padded)
    pad_start = pad_end - padded
    start = jnp.cumsum(counts) - counts
    dest = pad_start[e_sorted] + jnp.arange(n_assign, dtype=jnp.int32) - start[e_sorted]
    n_blk = -(-n_assign // MOE_BLOCK) + N_EXPERTS
    buf_tok = jnp.full((n_blk * MOE_BLOCK,), n_tok, jnp.int32).at[dest].set(flat_tok[order])
    buf_w = jnp.zeros((n_blk * MOE_BLOCK,), h.dtype).at[dest].set(flat_w[order].astype(h.dtype))
    blk_e = jnp.minimum(jnp.searchsorted(pad_end, jnp.arange(n_blk, dtype=jnp.int32) * MOE_BLOCK, side='right'), N_EXPERTS - 1)
    h_pad = jnp.concatenate([h, jnp.zeros((1, h.shape[1]), h.dtype)], axis=0)

    def expert_block(acc, blk):
        tok, wt, e = blk
        out = swiglu(h_pad[tok], w_exp_gate[e], w_exp_up[e], w_exp_down[e])
        return acc.at[tok].add(out * wt[:, None]), None

    routed, _ = lax.scan(expert_block, jnp.zeros_like(h_pad),
                         (buf_tok.reshape(n_blk, MOE_BLOCK), buf_w.reshape(n_blk, MOE_BLOCK), blk_e))
    return routed[:n_tok] + swiglu(h, w_sh_gate, w_sh_up, w_sh_down)


def trunk_layer(x, c, pos, past, lam_init, w_ada, b_ada, g_attn_pre, g_attn_post, g_ffn_pre, g_ffn_post,
                w_in, w_out, lambda_q1, lambda_k1, lambda_q2, lambda_k2, g_subln,
                w_router, b_router, w_exp_gate, w_exp_up, w_exp_down, w_sh_gate, w_sh_up, w_sh_down):
    mod = jnp.einsum('bd,de->be', jax.nn.silu(c), w_ada) + b_ada
    sh_a, sc_a, gt_a, sh_f, sc_f, gt_f = jnp.split(mod[:, None, :], 6, axis=-1)
    h = rmsnorm(x, g_attn_pre) * (1.0 + sc_a) + sh_a
    y, rows = token_mixers(h, pos, past, lam_init, w_in, w_out, lambda_q1, lambda_k1, lambda_q2, lambda_k2, g_subln)
    x = x + gt_a * rmsnorm(y, g_attn_post)
    h = rmsnorm(x, g_ffn_pre) * (1.0 + sc_f) + sh_f
    f = moe_ffn(h.reshape(-1, D_MODEL), w_router, b_router, w_exp_gate, w_exp_up, w_exp_down,
                w_sh_gate, w_sh_up, w_sh_down).reshape(x.shape)
    x = x + gt_f * rmsnorm(f, g_ffn_post)
    return x, rows


def gather_pages(cache, page_table):
    g = cache[page_table]
    return g.reshape((page_table.shape[0], page_table.shape[1] * PAGE_SIZE) + cache.shape[2:])


def setup_inputs(seed: int = 0) -> dict:
    key = jax.random.key(seed)
    ks = jax.random.split(key, 32)
    f32 = jnp.float32

    def nrm(k, shape, scale):
        return jax.random.normal(k, shape, f32) * scale

    n_pages = PAST_LEN // PAGE_SIZE
    n_pool = (DEC_BATCH * n_pages * 5) // 4
    page_table = jax.random.permutation(ks[0], n_pool)[: DEC_BATCH * n_pages].reshape(DEC_BATCH, n_pages).astype(jnp.int32)
    kd_shape = (DEPTH, n_pool, PAGE_SIZE, DIFF_HEADS, 2 * DIFF_HEAD_DIM)
    km_shape = (DEPTH, n_pool, PAGE_SIZE, MOBA_HEADS, MOBA_HEAD_DIM)
    return {
        'x_prompt': nrm(ks[1], (BATCH, SEQ, D_MODEL), 1.0),
        'x_sample': nrm(ks[2], (DEC_BATCH, DEC_SEQ, D_MODEL), 1.0),
        'c_prompt': nrm(ks[3], (BATCH, D_MODEL), 1.0),
        'c_sample': nrm(ks[4], (DEC_BATCH, D_MODEL), 1.0),
        'cache_k_diff': nrm(ks[5], kd_shape, 1.0),
        'cache_v_diff': nrm(ks[6], kd_shape, 1.0),
        'cache_k_moba': nrm(ks[7], km_shape, 1.0),
        'cache_v_moba': nrm(ks[8], km_shape, 1.0),
        'page_table': page_table,
        'w_ada': nrm(ks[9], (DEPTH, D_MODEL, 6 * D_MODEL), 0.5 * D_MODEL ** -0.5),
        'b_ada': nrm(ks[10], (DEPTH, 6 * D_MODEL), 0.02),
        'g_attn_pre': 1.0 + nrm(ks[11], (DEPTH, D_MODEL), 0.05),
        'g_attn_post': 1.0 + nrm(ks[12], (DEPTH, D_MODEL), 0.05),
        'g_ffn_pre': 1.0 + nrm(ks[13], (DEPTH, D_MODEL), 0.05),
        'g_ffn_post': 1.0 + nrm(ks[14], (DEPTH, D_MODEL), 0.05),
        'w_in': nrm(ks[15], (DEPTH, D_MODEL, D_IN), D_MODEL ** -0.5),
        'w_out': nrm(ks[16], (DEPTH, D_MIX, D_MODEL), D_MIX ** -0.5),
        'lambda_q1': nrm(ks[17], (DEPTH, DIFF_HEAD_DIM), 0.1),
        'lambda_k1': nrm(ks[18], (DEPTH, DIFF_HEAD_DIM), 0.1),
        'lambda_q2': nrm(ks[19], (DEPTH, DIFF_HEAD_DIM), 0.1),
        'lambda_k2': nrm(ks[20], (DEPTH, DIFF_HEAD_DIM), 0.1),
        'g_subln': 1.0 + nrm(ks[21], (DEPTH, 2 * DIFF_HEAD_DIM), 0.05),
        'w_router': nrm(ks[22], (DEPTH, D_MODEL, N_EXPERTS), D_MODEL ** -0.5),
        'b_router': nrm(ks[23], (DEPTH, N_EXPERTS), 0.01),
        'w_exp_gate': nrm(ks[24], (DEPTH, N_EXPERTS, D_MODEL, D_EXPERT), D_MODEL ** -0.5),
        'w_exp_up': nrm(ks[25], (DEPTH, N_EXPERTS, D_MODEL, D_EXPERT), D_MODEL ** -0.5),
        'w_exp_down': nrm(ks[26], (DEPTH, N_EXPERTS, D_EXPERT, D_MODEL), D_EXPERT ** -0.5),
        'w_sh_gate': nrm(ks[27], (DEPTH, D_MODEL, D_SHARED), D_MODEL ** -0.5),
        'w_sh_up': nrm(ks[28], (DEPTH, D_MODEL, D_SHARED), D_MODEL ** -0.5),
        'w_sh_down': nrm(ks[29], (DEPTH, D_SHARED, D_MODEL), D_SHARED ** -0.5),
    }


def reference(x_prompt, x_sample, c_prompt, c_sample, cache_k_diff, cache_v_diff, cache_k_moba, cache_v_moba,
              page_table, w_ada, b_ada, g_attn_pre, g_attn_post, g_ffn_pre, g_ffn_post, w_in, w_out,
              lambda_q1, lambda_k1, lambda_q2, lambda_k2, g_subln, w_router, b_router,
              w_exp_gate, w_exp_up, w_exp_down, w_sh_gate, w_sh_up, w_sh_down):
    pos_p = jnp.arange(SEQ, dtype=jnp.int32)
    pos_s = PAST_LEN + jnp.arange(DEC_SEQ, dtype=jnp.int32)
    yp, ys = x_prompt, x_sample
    rows_p, rows_s = [], []
    for l in range(DEPTH):
        lam_init = 0.8 - 0.6 * math.exp(-0.3 * l)
        layer_w = (w_ada[l], b_ada[l], g_attn_pre[l], g_attn_post[l], g_ffn_pre[l], g_ffn_post[l],
                   w_in[l], w_out[l], lambda_q1[l], lambda_k1[l], lambda_q2[l], lambda_k2[l], g_subln[l],
                   w_router[l], b_router[l], w_exp_gate[l], w_exp_up[l], w_exp_down[l],
                   w_sh_gate[l], w_sh_up[l], w_sh_down[l])
        yp, rp = trunk_layer(yp, c_prompt, pos_p, None, lam_init, *layer_w)
        past = (gather_pages(cache_k_diff[l], page_table), gather_pages(cache_v_diff[l], page_table),
                gather_pages(cache_k_moba[l], page_table), gather_pages(cache_v_moba[l], page_table))
        ys, rs = trunk_layer(ys, c_sample, pos_s, past, lam_init, *layer_w)
        rows_p.append(rp)
        rows_s.append(rs)
    k_diff_prompt = jnp.stack([r[0] for r in rows_p])
    v_diff_prompt = jnp.stack([r[1] for r in rows_p])
    k_moba_prompt = jnp.stack([r[2] for r in rows_p])
    v_moba_prompt = jnp.stack([r[3] for r in rows_p])
    k_diff_sample = jnp.stack([r[0] for r in rows_s])
    v_diff_sample = jnp.stack([r[1] for r in rows_s])
    k_moba_sample = jnp.stack([r[2] for r in rows_s])
    v_moba_sample = jnp.stack([r[3] for r in rows_s])
    return (yp, ys, k_diff_prompt, v_diff_prompt, k_moba_prompt, v_moba_prompt,
            k_diff_sample, v_diff_sample, k_moba_sample, v_moba_sample)
```

```python
import functools

import jax
import jax.numpy as jnp
from jax import lax
from jax.experimental import pallas as pl
from jax.experimental.pallas import tpu as pltpu

F32 = jnp.float32
BF16 = jnp.bfloat16
I32 = jnp.int32

HEAD_DIM = 64
ROT_DIMS = 16
ROPE_THETA = 500000.0
MOBA_BLOCK = 256
MOBA_TOPK = 3
N_GROUPS = 8
TOPK_GROUPS = 4
TOP_K = 8
ROUTED_SCALE = 2.5
MOE_BLOCK = 256
EPS = 1e-6
NEG = -1e30

LANES = 128
VMEM_LIMIT = 48 * 1024 * 1024


def _cparams(sem, vmem=VMEM_LIMIT):
    return pltpu.CompilerParams(dimension_semantics=sem, vmem_limit_bytes=vmem)


def _pick(n, candidates):
    for c in candidates:
        if c <= n and n % c == 0:
            return c
    return n


def _split_bf16(a):
    hi = a.astype(BF16)
    lo = (a - hi.astype(F32)).astype(BF16)
    return hi, lo


def _dot3(a, b, dims):
    ah, al = _split_bf16(a)
    bh, bl = _split_bf16(b)
    d = lambda x, y: lax.dot_general(x, y, (dims, ((), ())), preferred_element_type=F32)
    return d(ah, bh) + d(ah, bl) + d(al, bh)


_NN = ((1,), (0,))
_NT = ((1,), (1,))


def _rms(x, g):
    return x * lax.rsqrt(jnp.mean(x * x, axis=-1, keepdims=True) + EPS) * g


def _sigmoid(x):
    return 1.0 / (1.0 + jnp.exp(-x))


def _mod_kernel(c_ref, w_ref, b_ref, o_ref):
    c = c_ref[...]
    s = c * _sigmoid(c)
    o_ref[...] = _dot3(s, w_ref[...], _NN) + b_ref[...]


def _adaln(c, w_ada, b_ada):
    n, d = c.shape
    e = w_ada.shape[1]
    tn = _pick(e, (1024, 512, 256, 128))
    return pl.pallas_call(
        _mod_kernel,
        grid=(e // tn,),
        in_specs=[pl.BlockSpec((n, d), lambda j: (0, 0)),
                  pl.BlockSpec((d, tn), lambda j: (0, j)),
                  pl.BlockSpec((1, tn), lambda j: (0, j))],
        out_specs=pl.BlockSpec((n, tn), lambda j: (0, j)),
        out_shape=jax.ShapeDtypeStruct((n, e), F32),
        compiler_params=_cparams(("arbitrary",)),
    )(c, w_ada, b_ada.reshape(1, e))


def _rope_tables(pos):
    half = ROT_DIMS // 2
    inv_freq = ROPE_THETA ** (-2.0 * jnp.arange(half, dtype=F32) / ROT_DIMS)
    ang = pos.astype(F32)[:, None] * inv_freq[None, :]
    cos, sin = jnp.cos(ang), jnp.sin(ang)
    lane = jnp.arange(LANES) % HEAD_DIM
    idx = lane % half
    c = jnp.where(lane < ROT_DIMS, cos[:, idx], 1.0)
    sa = jnp.where(lane < half, -sin[:, idx], 0.0)
    sb = jnp.where((lane >= half) & (lane < ROT_DIMS), sin[:, idx], 0.0)
    return c.astype(F32), sa.astype(F32), sb.astype(F32)


def _rope(a, c, sa, sb):
    half = ROT_DIMS // 2
    out = []
    for j in range(a.shape[1] // LANES):
        x = a[:, j * LANES:(j + 1) * LANES]
        out.append(x * c + pltpu.roll(x, LANES - half, 1) * sa + pltpu.roll(x, half, 1) * sb)
    return jnp.concatenate(out, axis=1)


def _qkv_kernel(x_ref, sc_ref, sh_ref, g_ref, w_ref, c_ref, sa_ref, sb_ref, *outs, width, attn_layouts):
    x = x_ref[0]
    h = _rms(x, g_ref[...]) * (1.0 + sc_ref[0]) + sh_ref[0]
    proj = jnp.dot(h.astype(BF16), w_ref[...], preferred_element_type=F32)
    c, sa, sb = c_ref[...], sa_ref[...], sb_ref[...]
    w = width
    scale = HEAD_DIM ** -0.5
    qd = _rope(proj[:, 0 * w:1 * w], c, sa, sb) * scale
    kd = _rope(proj[:, 1 * w:2 * w], c, sa, sb)
    vd = proj[:, 2 * w:3 * w]
    qm = _rope(proj[:, 3 * w:4 * w], c, sa, sb) * scale
    km = _rope(proj[:, 4 * w:5 * w], c, sa, sb)
    vm = proj[:, 5 * w:6 * w]
    if not attn_layouts:
        for r, v in zip(outs, (qd, kd, vd, qm, km, vm)):
            r[0] = v
        return
    kd_o, vd_o, km_o, vm_o, qm_o, qdT_o, qmT_o, kdh_o, kmh_o, vdT_o, vmT_o = outs
    kd_o[0] = kd
    vd_o[0] = vd
    km_o[0] = km
    vm_o[0] = vm
    qm_o[0] = qm
    qdT_o[0] = qd.T.astype(BF16)
    qmT_o[0] = qm.T.astype(BF16)
    for m in range(w // HEAD_DIM):
        kdh_o[0, m] = kd[:, m * HEAD_DIM:(m + 1) * HEAD_DIM].astype(BF16)
        kmh_o[0, m] = km[:, m * HEAD_DIM:(m + 1) * HEAD_DIM].astype(BF16)
    for r in range(x.shape[0] // MOBA_BLOCK):
        vdT_o[0, r] = vd[r * MOBA_BLOCK:(r + 1) * MOBA_BLOCK].T.astype(BF16)
        vmT_o[0, r] = vm[r * MOBA_BLOCK:(r + 1) * MOBA_BLOCK].T.astype(BF16)


def _qkv(x, sc, sh, g, w_in16, tabs, attn_layouts):
    b, s, d = x.shape
    w = w_in16.shape[1] // 6
    ts = _pick(s, (512, 256)) if attn_layouts else s
    sm = sc.shape[1]
    mod_spec = pl.BlockSpec((1, 1 if sm == 1 else ts, d), (lambda i, j: (i, 0, 0)) if sm == 1 else (lambda i, j: (i, j, 0)))
    tab_spec = pl.BlockSpec((ts, LANES), lambda i, j: (j, 0))
    row_spec = pl.BlockSpec((1, ts, w), lambda i, j: (i, j, 0))
    row_shape = jax.ShapeDtypeStruct((b, s, w), F32)
    if attn_layouts:
        nb = s // MOBA_BLOCK
        nh = w // HEAD_DIM
        out_specs = [row_spec] * 5 + [
            pl.BlockSpec((1, w, ts), lambda i, j: (i, 0, j)),
            pl.BlockSpec((1, w, ts), lambda i, j: (i, 0, j)),
            pl.BlockSpec((1, nh, ts, HEAD_DIM), lambda i, j: (i, 0, j, 0)),
            pl.BlockSpec((1, nh, ts, HEAD_DIM), lambda i, j: (i, 0, j, 0)),
            pl.BlockSpec((1, ts // MOBA_BLOCK, w, MOBA_BLOCK), lambda i, j: (i, j, 0, 0)),
            pl.BlockSpec((1, ts // MOBA_BLOCK, w, MOBA_BLOCK), lambda i, j: (i, j, 0, 0)),
        ]
        out_shape = [row_shape] * 5 + [
            jax.ShapeDtypeStruct((b, w, s), BF16), jax.ShapeDtypeStruct((b, w, s), BF16),
            jax.ShapeDtypeStruct((b, nh, s, HEAD_DIM), BF16), jax.ShapeDtypeStruct((b, nh, s, HEAD_DIM), BF16),
            jax.ShapeDtypeStruct((b, nb, w, MOBA_BLOCK), BF16), jax.ShapeDtypeStruct((b, nb, w, MOBA_BLOCK), BF16),
        ]
    else:
        out_specs = [row_spec] * 6
        out_shape = [row_shape] * 6
    return pl.pallas_call(
        functools.partial(_qkv_kernel, width=w, attn_layouts=attn_layouts),
        grid=(b, s // ts),
        in_specs=[pl.BlockSpec((1, ts, d), lambda i, j: (i, j, 0)), mod_spec, mod_spec,
                  pl.BlockSpec((1, d), lambda i, j: (0, 0)),
                  pl.BlockSpec(w_in16.shape, lambda i, j: (0, 0)),
                  tab_spec, tab_spec, tab_spec],
        out_specs=out_specs,
        out_shape=out_shape,
        compiler_params=_cparams(("arbitrary", "arbitrary")),
    )(x, sc, sh, g.reshape(1, d), w_in16, *tabs)


def _select_kernel(q_ref, k_ref, o_ref, kmean_ref, *, nb, nh):
    j = pl.program_id(1)

    @pl.when(j == 0)
    def _():
        kmean_ref[...] = jnp.zeros_like(kmean_ref)

    kmean_ref[pl.ds(j, 1), :] = jnp.mean(k_ref[0], axis=0, keepdims=True)
    kmean = kmean_ref[...]
    lane_head = lax.broadcasted_iota(I32, kmean.shape, 1) // HEAD_DIM
    kbd = jnp.concatenate([jnp.where(lane_head == h, kmean, 0.0) for h in range(nh)], axis=0)
    gate = _dot3(kbd, q_ref[0], _NT).reshape(nh, nb, -1)
    row = lax.broadcasted_iota(I32, gate.shape, 1)
    rank = jnp.zeros(gate.shape, F32)
    for jp in range(nb):
        other = gate[:, jp:jp + 1, :]
        beats = (other > gate) | ((other == gate) & (jp < row))
        rank = rank + jnp.where(beats & (jp < j), 1.0, 0.0)
    o_ref[0] = jnp.where((row < j) & (rank < MOBA_TOPK), 1.0, 0.0)


def _moba_select(qm, km):
    b, s, w = qm.shape
    nb = s // MOBA_BLOCK
    nh = w // HEAD_DIM
    return pl.pallas_call(
        functools.partial(_select_kernel, nb=nb, nh=nh),
        grid=(b, nb),
        in_specs=[pl.BlockSpec((1, MOBA_BLOCK, w), lambda i, j: (i, j, 0)),
                  pl.BlockSpec((1, MOBA_BLOCK, w), lambda i, j: (i, j, 0))],
        out_specs=pl.BlockSpec((1, nh, nb, MOBA_BLOCK), lambda i, j: (i, 0, 0, j)),
        out_shape=jax.ShapeDtypeStruct((b, nh, nb, s), F32),
        scratch_shapes=[pltpu.VMEM((nb, w), F32)],
        compiler_params=_cparams(("arbitrary", "arbitrary")),
    )(qm, km)


def _flash_kernel(*refs, diff, lam_init):
    if diff:
        q_ref, k_ref, v_ref, lam_ref, g_ref, o_ref = refs
        sel_ref = None
    else:
        q_ref, k_ref, v_ref, sel_ref, o_ref = refs
    i = pl.program_id(2)
    tq = q_ref.shape[2]
    tk = MOBA_BLOCK
    dv = v_ref.shape[2] if diff else HEAD_DIM
    qs = (q_ref[0, 0:HEAD_DIM, :], q_ref[0, HEAD_DIM:2 * HEAD_DIM, :])

    def step(j, carry, diagonal):
        new = []
        for s in range(2):
            m, l, acc = carry[s]
            k = k_ref[0, s, pl.ds(pl.multiple_of(j * tk, tk), tk), :]
            st = jnp.dot(k, qs[s], preferred_element_type=F32)
            if diagonal:
                keep = lax.broadcasted_iota(I32, st.shape, 0) <= lax.broadcasted_iota(I32, st.shape, 1)
                st = jnp.where(keep, st, NEG)
            elif sel_ref is not None:
                st = jnp.where(sel_ref[0, s, pl.ds(j, 1), :] > 0.0, st, NEG)
            m_new = jnp.maximum(m, jnp.max(st, axis=0, keepdims=True))
            alpha = jnp.exp(m - m_new)
            p = jnp.exp(st - m_new)
            l = alpha * l + jnp.sum(p, axis=0, keepdims=True)
            v = v_ref[0, j] if diff else v_ref[0, j, s * HEAD_DIM:(s + 1) * HEAD_DIM, :]
            acc = alpha * acc + jnp.dot(v, p.astype(BF16), preferred_element_type=F32)
            new.append((m_new, l, acc))
        return tuple(new)

    init = tuple((jnp.full((1, tq), NEG, F32), jnp.zeros((1, tq), F32), jnp.zeros((dv, tq), F32)) for _ in range(2))
    carry = lax.fori_loop(0, i, lambda j, c: step(j, c, False), init)
    (m0, l0, a0), (m1, l1, a1) = step(i, carry, True)
    o0 = a0 / l0
    o1 = a1 / l1
    if diff:
        lv = lam_ref[...]
        lam = (jnp.exp(jnp.sum(lv[0:1] * lv[1:2], axis=1, keepdims=True))
               - jnp.exp(jnp.sum(lv[2:3] * lv[3:4], axis=1, keepdims=True)) + lam_init)
        o = o0 - lam * o1
        o = o * lax.rsqrt(jnp.mean(o * o, axis=0, keepdims=True) + EPS) * g_ref[...] * (1.0 - lam_init)
    else:
        o = jnp.concatenate([o0, o1], axis=0)
    o_ref[0] = o.T.astype(o_ref.dtype)


def _flash(qT, k_hm, vT, extra, diff, lam_init):
    b, w, s = qT.shape
    nb = s // MOBA_BLOCK
    tq = MOBA_BLOCK
    ng = w // (2 * HEAD_DIM)
    in_specs = [pl.BlockSpec((1, 2 * HEAD_DIM, tq), lambda bi, g, i: (bi, g, i)),
                pl.BlockSpec((1, 2, s, HEAD_DIM), lambda bi, g, i: (bi, g, 0, 0)),
                pl.BlockSpec((1, nb, 2 * HEAD_DIM, MOBA_BLOCK), lambda bi, g, i: (bi, 0, g, 0))]
    if diff:
        lam_vecs, g_sub = extra
        in_specs += [pl.BlockSpec(lam_vecs.shape, lambda bi, g, i: (0, 0)),
                     pl.BlockSpec(g_sub.shape, lambda bi, g, i: (0, 0))]
        args = (lam_vecs, g_sub)
    else:
        (sel,) = extra
        in_specs += [pl.BlockSpec((1, 2, nb, tq), lambda bi, g, i: (bi, g, 0, i))]
        args = (sel,)
    return pl.pallas_call(
        functools.partial(_flash_kernel, diff=diff, lam_init=lam_init),
        grid=(b, ng, s // tq),
        in_specs=in_specs,
        out_specs=pl.BlockSpec((1, tq, 2 * HEAD_DIM), lambda bi, g, i: (bi, i, g)),
        out_shape=jax.ShapeDtypeStruct((b, s, w), BF16),
        compiler_params=_cparams(("arbitrary", "arbitrary", "arbitrary")),
    )(qT, k_hm, vT, *args)


def _post_kernel(od_ref, om_ref, x_ref, gt_ref, sc_ref, sh_ref, ga_ref, gf_ref, w_ref, wr_ref,
                 x1_ref, h2_ref, sc_out_ref):
    w = od_ref.shape[2]
    y = (jnp.dot(od_ref[0].astype(BF16), w_ref[0:w, :], preferred_element_type=F32)
         + jnp.dot(om_ref[0].astype(BF16), w_ref[w:2 * w, :], preferred_element_type=F32))
    x1 = x_ref[0] + gt_ref[0] * _rms(y, ga_ref[...])
    h2 = _rms(x1, gf_ref[...]) * (1.0 + sc_ref[0]) + sh_ref[0]
    x1_ref[0] = x1
    h2_ref[0] = h2
    sc_out_ref[...] = _sigmoid(_dot3(wr_ref[...], h2, _NT))


def _post_attn(od, om, x, gt, sc, sh, g_post, g_pre, w_out16, w_rT):
    b, s, d = x.shape
    w = od.shape[2]
    ne = w_rT.shape[0]
    ts = _pick(s, (512, 256, 128))
    sm = gt.shape[1]
    mod_spec = pl.BlockSpec((1, 1 if sm == 1 else ts, d), (lambda i, j: (i, 0, 0)) if sm == 1 else (lambda i, j: (i, j, 0)))
    vec_spec = pl.BlockSpec((1, d), lambda i, j: (0, 0))
    tok_spec = pl.BlockSpec((1, ts, d), lambda i, j: (i, j, 0))
    ns = s // ts
    return pl.pallas_call(
        _post_kernel,
        grid=(b, ns),
        in_specs=[pl.BlockSpec((1, ts, w), lambda i, j: (i, j, 0)), pl.BlockSpec((1, ts, w), lambda i, j: (i, j, 0)),
                  tok_spec, mod_spec, mod_spec, mod_spec, vec_spec, vec_spec,
                  pl.BlockSpec(w_out16.shape, lambda i, j: (0, 0)),
                  pl.BlockSpec(w_rT.shape, lambda i, j: (0, 0))],
        out_specs=[tok_spec, tok_spec, pl.BlockSpec((ne, ts), lambda i, j: (0, i * ns + j))],
        out_shape=[jax.ShapeDtypeStruct((b, s, d), F32), jax.ShapeDtypeStruct((b, s, d), F32),
                   jax.ShapeDtypeStruct((ne, b * s), F32)],
        compiler_params=_cparams(("arbitrary", "arbitrary")),
    )(od, om, x, gt, sc, sh, g_post.reshape(1, d), g_pre.reshape(1, d), w_out16, w_rT)


def _take_max(x, iota, fill):
    m = jnp.max(x, axis=0, keepdims=True)
    idx = jnp.min(jnp.where(x == m, iota, fill), axis=0, keepdims=True)
    hot = iota == idx
    return m, idx, hot, jnp.where(hot, -jnp.inf, x)


def _route_kernel(s_ref, b_ref, tri_ref, idx_ref, pos_ref, w_ref, cnt_ref, run_ref):
    step = pl.program_id(0)

    @pl.when(step == 0)
    def _():
        run_ref[...] = jnp.zeros_like(run_ref)

    scores = s_ref[...]
    ne, tn = scores.shape
    gs = ne // N_GROUPS
    biased = scores + b_ref[...]
    grp = biased.reshape(N_GROUPS, gs, tn)
    sub = lax.broadcasted_iota(I32, grp.shape, 1)
    m1 = jnp.max(grp, axis=1, keepdims=True)
    i1 = jnp.min(jnp.where(grp == m1, sub, gs), axis=1, keepdims=True)
    m2 = jnp.max(jnp.where(sub == i1, -jnp.inf, grp), axis=1, keepdims=True)
    gscore = (m1 + m2).reshape(N_GROUPS, tn)
    giota = lax.broadcasted_iota(I32, gscore.shape, 0)
    gmask = jnp.zeros(gscore.shape, jnp.bool_)
    for _ in range(TOPK_GROUPS):
        _, _, hot, gscore = _take_max(gscore, giota, N_GROUPS)
        gmask = gmask | hot
    emask = jnp.broadcast_to(gmask.reshape(N_GROUPS, 1, tn), (N_GROUPS, gs, tn)).reshape(ne, tn)
    cand = jnp.where(emask, biased, -jnp.inf)
    eiota = lax.broadcasted_iota(I32, cand.shape, 0)
    chosen = jnp.zeros(cand.shape, F32)
    idxs, wts = [], []
    for _ in range(TOP_K):
        _, idx, hot, cand = _take_max(cand, eiota, ne)
        idxs.append(idx)
        wts.append(jnp.sum(jnp.where(hot, scores, 0.0), axis=0, keepdims=True))
        chosen = chosen + jnp.where(hot, 1.0, 0.0)
    wsum = wts[0]
    for t in wts[1:]:
        wsum = wsum + t
    before = run_ref[...] + jnp.dot(chosen.astype(BF16), tri_ref[...], preferred_element_type=F32)
    poss = [jnp.sum(jnp.where(eiota == idx, before, 0.0), axis=0, keepdims=True) for idx in idxs]
    idx_ref[...] = jnp.concatenate(idxs, axis=0)
    pos_ref[...] = jnp.concatenate(poss, axis=0).astype(I32)
    w_ref[...] = jnp.concatenate(wts, axis=0) / wsum * ROUTED_SCALE
    run = run_ref[...] + jnp.sum(chosen, axis=1, keepdims=True)
    run_ref[...] = run
    cnt_ref[...] = jnp.broadcast_to(run, cnt_ref.shape)


def _route(scoresT, b_router):
    ne, n = scoresT.shape
    tn = _pick(n, (1152, 1024, 512, 384, 256, 128))
    tri = jnp.triu(jnp.ones((tn, tn), BF16), k=1)
    blk = pl.BlockSpec((TOP_K, tn), lambda i: (0, i))
    return pl.pallas_call(
        _route_kernel,
        grid=(n // tn,),
        in_specs=[pl.BlockSpec((ne, tn), lambda i: (0, i)),
                  pl.BlockSpec((ne, 1), lambda i: (0, 0)),
                  pl.BlockSpec((tn, tn), lambda i: (0, 0))],
        out_specs=[blk, blk, blk, pl.BlockSpec((ne, LANES), lambda i: (0, 0))],
        out_shape=[jax.ShapeDtypeStruct((TOP_K, n), I32), jax.ShapeDtypeStruct((TOP_K, n), I32),
                   jax.ShapeDtypeStruct((TOP_K, n), F32), jax.ShapeDtypeStruct((ne, LANES), F32)],
        scratch_shapes=[pltpu.VMEM((ne, 1), F32)],
        compiler_params=_cparams(("arbitrary",)),
    )(scoresT, b_router.reshape(ne, 1), tri)


def _row_copy(src, si, dst, di, sem):
    return pltpu.make_async_copy(src.at[pl.ds(si, 1)], dst.at[pl.ds(di, 1)], sem)


def _dispatch_kernel(dest_ref, h_ref, xs_in_ref, xs_ref, sem):
    del xs_in_ref
    tn = h_ref.shape[0]

    def issue(t, _):
        for k in range(TOP_K):
            _row_copy(h_ref, t, xs_ref, dest_ref[k, t], sem).start()
        return _

    lax.fori_loop(0, tn, issue, 0)

    def drain(t, _):
        for k in range(TOP_K):
            _row_copy(h_ref, t, xs_ref, dest_ref[k, t], sem).wait()
        return _

    lax.fori_loop(0, tn, drain, 0)


def _dispatch(dest, h2, n_rows):
    n, d = h2.shape
    tn = _pick(n, (128,))
    xs0 = jnp.zeros((n_rows, d), F32)
    return pl.pallas_call(
        _dispatch_kernel,
        grid=(n // tn,),
        in_specs=[pl.BlockSpec((TOP_K, tn), lambda i: (0, i), memory_space=pltpu.SMEM),
                  pl.BlockSpec((tn, d), lambda i: (i, 0)),
                  pl.BlockSpec(memory_space=pl.ANY)],
        out_specs=pl.BlockSpec(memory_space=pl.ANY),
        out_shape=jax.ShapeDtypeStruct((n_rows, d), F32),
        scratch_shapes=[pltpu.SemaphoreType.DMA],
        input_output_aliases={2: 0},
        compiler_params=_cparams(("arbitrary",)),
    )(dest, h2, xs0)


def _expert_kernel(be_ref, x_ref, wgu_ref, wd_ref, o_ref):
    del be_ref
    de = wd_ref.shape[1]
    gu = jnp.dot(x_ref[...].astype(BF16), wgu_ref[0], preferred_element_type=F32)
    g, u = gu[:, :de], gu[:, de:]
    mid = g * _sigmoid(g) * u
    o_ref[...] = jnp.dot(mid.astype(BF16), wd_ref[0], preferred_element_type=F32)


def _experts(blk_e, xs, wgu16, wd16):
    n_rows, d = xs.shape
    de = wd16.shape[1]
    return pl.pallas_call(
        _expert_kernel,
        grid_spec=pltpu.PrefetchScalarGridSpec(
            num_scalar_prefetch=1,
            grid=(n_rows // MOE_BLOCK,),
            in_specs=[pl.BlockSpec((MOE_BLOCK, d), lambda i, be: (i, 0)),
                      pl.BlockSpec((1, d, 2 * de), lambda i, be: (be[i], 0, 0)),
                      pl.BlockSpec((1, de, d), lambda i, be: (be[i], 0, 0))],
            out_specs=pl.BlockSpec((MOE_BLOCK, d), lambda i, be: (i, 0))),
        out_shape=jax.ShapeDtypeStruct((n_rows, d), F32),
        compiler_params=_cparams(("arbitrary",)),
    )(blk_e, xs, wgu16, wd16)


def _combine_kernel(dest_ref, ys_ref, w_ref, h_ref, x1_ref, gt_ref, g_ref, wsgu_ref, wsd_ref, o_ref, rows_ref, sem):
    tn = h_ref.shape[0]

    def issue(t, _):
        for k in range(TOP_K):
            _row_copy(ys_ref, dest_ref[k, t], rows_ref.at[k], t, sem).start()
        return _

    lax.fori_loop(0, tn, issue, 0)
    de = wsd_ref.shape[0]
    gu = jnp.dot(h_ref[...].astype(BF16), wsgu_ref[...], preferred_element_type=F32)
    g, u = gu[:, :de], gu[:, de:]
    f = jnp.dot((g * _sigmoid(g) * u).astype(BF16), wsd_ref[...], preferred_element_type=F32)

    def drain(t, _):
        for k in range(TOP_K):
            _row_copy(ys_ref, dest_ref[k, t], rows_ref.at[k], t, sem).wait()
        return _

    lax.fori_loop(0, tn, drain, 0)
    w = w_ref[...]
    for k in range(TOP_K):
        f = f + rows_ref[k] * w[:, k:k + 1]
    o_ref[0] = x1_ref[0] + gt_ref[0] * _rms(f, g_ref[...])


def _combine(dest, ys, wts, h2, x1, gt, g_post, wsgu16, wsd16, tok0):
    b, s, d = x1.shape
    tn = _pick(s, (128,))
    ns = s // tn
    off = tok0 // tn
    sm = gt.shape[1]
    mod_spec = pl.BlockSpec((1, 1 if sm == 1 else tn, d), (lambda i, j: (i, 0, 0)) if sm == 1 else (lambda i, j: (i, j, 0)))
    return pl.pallas_call(
        _combine_kernel,
        grid=(b, ns),
        in_specs=[pl.BlockSpec((TOP_K, tn), lambda i, j: (0, off + i * ns + j), memory_space=pltpu.SMEM),
                  pl.BlockSpec(memory_space=pl.ANY),
                  pl.BlockSpec((tn, TOP_K), lambda i, j: (off + i * ns + j, 0)),
                  pl.BlockSpec((tn, d), lambda i, j: (off + i * ns + j, 0)),
                  pl.BlockSpec((1, tn, d), lambda i, j: (i, j, 0)),
                  mod_spec,
                  pl.BlockSpec((1, d), lambda i, j: (0, 0)),
                  pl.BlockSpec(wsgu16.shape, lambda i, j: (0, 0)),
                  pl.BlockSpec(wsd16.shape, lambda i, j: (0, 0))],
        out_specs=pl.BlockSpec((1, tn, d), lambda i, j: (i, j, 0)),
        out_shape=jax.ShapeDtypeStruct((b, s, d), F32),
        scratch_shapes=[pltpu.VMEM((TOP_K, tn, d), F32), pltpu.SemaphoreType.DMA],
        compiler_params=_cparams(("arbitrary", "arbitrary")),
    )(dest, ys, wts, h2, x1, gt, g_post.reshape(1, d), wsgu16, wsd16)


PAGES_PER_STEP = 8


def _dec_diff_kernel(pt_ref, q_ref, kn_ref, vn_ref, lam_ref, g_ref, *refs, lam_init, nmaps):
    del pt_ref
    k_refs = refs[:PAGES_PER_STEP]
    v_refs = refs[PAGES_PER_STEP:2 * PAGES_PER_STEP]
    o_ref, m_ref, l_ref, acc_ref = refs[2 * PAGES_PER_STEP:]
    p = pl.program_id(1)
    w = q_ref.shape[2]
    q = q_ref[0]
    lane_map = lax.broadcasted_iota(I32, (nmaps, w), 1) // HEAD_DIM
    row_map = lax.broadcasted_iota(I32, (nmaps, w), 0)
    qbd = jnp.where(lane_map == row_map, jnp.broadcast_to(q, (nmaps, w)), 0.0)
    qbd16 = qbd.astype(BF16)

    @pl.when(p == 0)
    def _():
        s_new = jnp.sum(qbd * kn_ref[0], axis=1, keepdims=True)
        m_ref[...] = s_new
        l_ref[...] = jnp.ones_like(l_ref)
        acc_ref[...] = jnp.broadcast_to(vn_ref[0], acc_ref.shape)

    st = jnp.concatenate(
        [lax.dot_general(qbd16, k_refs[r][0].astype(BF16), (_NT, ((), ())), preferred_element_type=F32)
         for r in range(PAGES_PER_STEP)], axis=1)
    m_old = m_ref[...]
    m_new = jnp.maximum(m_old, jnp.max(st, axis=1, keepdims=True))
    alpha = jnp.exp(m_old - m_new)
    pr = jnp.exp(st - m_new)
    l_ref[...] = alpha * l_ref[...] + jnp.sum(pr, axis=1, keepdims=True)
    pr16 = pr.astype(BF16)
    page = k_refs[0].shape[1]
    pv = jnp.zeros(acc_ref.shape, F32)
    for r in range(PAGES_PER_STEP):
        pv = pv + jnp.dot(pr16[:, r * page:(r + 1) * page], v_refs[r][0].astype(BF16), preferred_element_type=F32)
    acc_ref[...] = alpha * acc_ref[...] + pv
    m_ref[...] = m_new

    @pl.when(p == pl.num_programs(1) - 1)
    def _():
        o = acc_ref[...] / l_ref[...]
        lv = lam_ref[...]
        lam = (jnp.exp(jnp.sum(lv[0:1] * lv[1:2], axis=1, keepdims=True))
               - jnp.exp(jnp.sum(lv[2:3] * lv[3:4], axis=1, keepdims=True)) + lam_init)
        dv = 2 * HEAD_DIM
        outs = []
        for h in range(nmaps // 2):
            d = o[2 * h:2 * h + 1, h * dv:(h + 1) * dv] - lam * o[2 * h + 1:2 * h + 2, h * dv:(h + 1) * dv]
            outs.append(_rms(d, g_ref[...]) * (1.0 - lam_init))
        o_ref[0] = jnp.concatenate(outs, axis=1)


def _dec_diff(page_table, q, k_new, v_new, cache_k, cache_v, lam_vecs, g_sub, lam_init):
    nseq, w = q.shape
    npages = page_table.shape[1]
    page = cache_k.shape[1]
    nmaps = w // HEAD_DIM
    nsteps = npages // PAGES_PER_STEP

    def page_spec(r):
        return pl.BlockSpec((1, page, w), lambda b, p, pt: (pt[b, p * PAGES_PER_STEP + r], 0, 0))

    seq_spec = pl.BlockSpec((1, 1, w), lambda b, p, pt: (b, 0, 0))
    return pl.pallas_call(
        functools.partial(_dec_diff_kernel, lam_init=lam_init, nmaps=nmaps),
        grid_spec=pltpu.PrefetchScalarGridSpec(
            num_scalar_prefetch=1,
            grid=(nseq, nsteps),
            in_specs=[seq_spec, seq_spec, seq_spec,
                      pl.BlockSpec(lam_vecs.shape, lambda b, p, pt: (0, 0)),
                      pl.BlockSpec((1, 2 * HEAD_DIM), lambda b, p, pt: (0, 0))]
                     + [page_spec(r) for r in range(PAGES_PER_STEP)] * 2,
            out_specs=seq_spec,
            scratch_shapes=[pltpu.VMEM((nmaps, 1), F32), pltpu.VMEM((nmaps, 1), F32), pltpu.VMEM((nmaps, w), F32)]),
        out_shape=jax.ShapeDtypeStruct((nseq, 1, w), F32),
        compiler_params=_cparams(("arbitrary", "arbitrary")),
    )(page_table, q.reshape(nseq, 1, w), k_new.reshape(nseq, 1, w), v_new.reshape(nseq, 1, w),
      lam_vecs, g_sub.reshape(1, 2 * HEAD_DIM), *([cache_k] * PAGES_PER_STEP), *([cache_v] * PAGES_PER_STEP))


def _dec_gate_kernel(pt_ref, q_ref, *refs, nh, pages_per_block):
    del pt_ref
    k_refs = refs[:PAGES_PER_STEP]
    o_ref, kmean_ref = refs[PAGES_PER_STEP:]
    p = pl.program_id(1)
    bps = PAGES_PER_STEP // pages_per_block
    for r in range(bps):
        tot = jnp.sum(k_refs[r * pages_per_block][0], axis=0, keepdims=True)
        for e in range(1, pages_per_block):
            tot = tot + jnp.sum(k_refs[r * pages_per_block + e][0], axis=0, keepdims=True)
        kmean_ref[pl.ds(p * bps + r, 1), :] = tot * (1.0 / MOBA_BLOCK)

    @pl.when(p == pl.num_programs(1) - 1)
    def _():
        kmean = kmean_ref[...]
        nb, w = kmean.shape
        q = q_ref[0]
        lane_head = lax.broadcasted_iota(I32, (nh, w), 1) // HEAD_DIM
        row_head = lax.broadcasted_iota(I32, (nh, w), 0)
        qbd = jnp.where(lane_head == row_head, jnp.broadcast_to(q, (nh, w)), 0.0)
        gate = _dot3(qbd, kmean, _NT)
        biota = lax.broadcasted_iota(I32, gate.shape, 1)
        lane = lax.broadcasted_iota(I32, (nh, LANES), 1)
        out = jnp.zeros((nh, LANES), I32)
        for r in range(MOBA_TOPK):
            m = jnp.max(gate, axis=1, keepdims=True)
            idx = jnp.min(jnp.where(gate == m, biota, nb), axis=1, keepdims=True)
            out = jnp.where(lane == r, idx, out)
            gate = jnp.where(biota == idx, -jnp.inf, gate)
        o_ref[0] = out


def _dec_gate(page_table, q, cache_k):
    nseq, w = q.shape
    npages = page_table.shape[1]
    page = cache_k.shape[1]
    nh = w // HEAD_DIM
    ppb = MOBA_BLOCK // page
    nblocks = npages // ppb

    def page_spec(r):
        return pl.BlockSpec((1, page, w), lambda b, p, pt: (pt[b, p * PAGES_PER_STEP + r], 0, 0))

    return pl.pallas_call(
        functools.partial(_dec_gate_kernel, nh=nh, pages_per_block=ppb),
        grid_spec=pltpu.PrefetchScalarGridSpec(
            num_scalar_prefetch=1,
            grid=(nseq, npages // PAGES_PER_STEP),
            in_specs=[pl.BlockSpec((1, 1, w), lambda b, p, pt: (b, 0, 0))]
                     + [page_spec(r) for r in range(PAGES_PER_STEP)],
            out_specs=pl.BlockSpec((1, nh, LANES), lambda b, p, pt: (b, 0, 0)),
            scratch_shapes=[pltpu.VMEM((nblocks, w), F32)]),
        out_shape=jax.ShapeDtypeStruct((nseq, nh, LANES), I32),
        compiler_params=_cparams(("arbitrary", "arbitrary")),
    )(page_table, q.reshape(nseq, 1, w), *([cache_k] * PAGES_PER_STEP))


def _dec_moba_kernel(pt_ref, sel_ref, q_ref, kn_ref, vn_ref, *refs, pages_per_block):
    del pt_ref, sel_ref
    npg = MOBA_TOPK * pages_per_block
    o_ref = refs[-1]
    pair = 2 * HEAD_DIM
    q = q_ref[0]
    lane_head = lax.broadcasted_iota(I32, (8, pair), 1) // HEAD_DIM
    outs = []
    for hh in range(2):
        k_refs = refs[hh * 2 * npg: hh * 2 * npg + npg]
        v_refs = refs[hh * 2 * npg + npg: (hh + 1) * 2 * npg]
        qrow = jnp.where((lane_head == hh) & (lax.broadcasted_iota(I32, (8, pair), 0) == 0),
                         jnp.broadcast_to(q, (8, pair)), 0.0)
        q16 = qrow.astype(BF16)
        st = jnp.concatenate(
            [lax.dot_general(q16, k_refs[r][0].astype(BF16), (_NT, ((), ())), preferred_element_type=F32)
             for r in range(npg)], axis=1)
        s_new = jnp.sum(qrow * kn_ref[0], axis=1, keepdims=True)
        m = jnp.maximum(jnp.max(st, axis=1, keepdims=True), s_new)
        pr = jnp.exp(st - m)
        p_new = jnp.exp(s_new - m)
        l = jnp.sum(pr, axis=1, keepdims=True) + p_new
        pr16 = pr.astype(BF16)
        page = k_refs[0].shape[1]
        acc = p_new * vn_ref[0]
        for r in range(npg):
            acc = acc + jnp.dot(pr16[:, r * page:(r + 1) * page], v_refs[r][0].astype(BF16), preferred_element_type=F32)
        outs.append((acc / l)[0:1, hh * HEAD_DIM:(hh + 1) * HEAD_DIM])
    o_ref[0] = jnp.concatenate(outs, axis=1)


def _dec_moba(page_table, sel, q, k_new, v_new, cache_k, cache_v):
    nseq, w = q.shape
    page = cache_k.shape[1]
    ppb = MOBA_BLOCK // page
    pair = 2 * HEAD_DIM
    npairs = w // pair
    npg = MOBA_TOPK * ppb

    def page_spec(hh, r):
        def imap(b, g, pt, sl):
            blk = sl[b, (2 * g + hh) * MOBA_TOPK + r // ppb]
            return (pt[b, blk * ppb + r % ppb], 0, g)
        return pl.BlockSpec((1, page, pair), imap)

    seq_spec = pl.BlockSpec((1, 1, pair), lambda b, g, pt, sl: (b, 0, g))
    page_specs = []
    operands = []
    for hh in range(2):
        page_specs += [page_spec(hh, r) for r in range(npg)] * 2
        operands += [cache_k] * npg + [cache_v] * npg
    return pl.pallas_call(
        functools.partial(_dec_moba_kernel, pages_per_block=ppb),
        grid_spec=pltpu.PrefetchScalarGridSpec(
            num_scalar_prefetch=2,
            grid=(nseq, npairs),
            in_specs=[seq_spec, seq_spec, seq_spec] + page_specs,
            out_specs=seq_spec),
        out_shape=jax.ShapeDtypeStruct((nseq, 1, w), F32),
        compiler_params=_cparams(("arbitrary", "arbitrary")),
    )(page_table, sel, q.reshape(nseq, 1, w), k_new.reshape(nseq, 1, w), v_new.reshape(nseq, 1, w), *operands)


def _layer(xp, xs, cp, cs, ckd, cvd, ckm, cvm, page_table, lam_init, w_ada, b_ada, g_attn_pre, g_attn_post,
           g_ffn_pre, g_ffn_post, w_in, w_out, lq1, lk1, lq2, lk2, g_subln, w_router, b_router,
           w_exp_gate, w_exp_up, w_exp_down, w_sh_gate, w_sh_up, w_sh_down):
    b, s, d = xp.shape
    nseq = xs.shape[0]
    past = page_table.shape[1] * ckd.shape[1]
    w = w_in.shape[1] // 6
    assert s % MOBA_BLOCK == 0 and MOBA_BLOCK % ckd.shape[1] == 0
    assert page_table.shape[1] % PAGES_PER_STEP == 0 and past // MOBA_BLOCK >= MOBA_TOPK and past % MOBA_BLOCK == 0
    assert xs.shape[1] == 1 and nseq % LANES == 0 and (b * s) % LANES == 0

    mod = _adaln(jnp.concatenate([cp, cs], axis=0), w_ada, b_ada)
    mod_p = [m[:, None, :] for m in jnp.split(mod[:b], 6, axis=-1)]
    mod_s = [m[None, :, :] for m in jnp.split(mod[b:], 6, axis=-1)]

    w_in16 = w_in.astype(BF16)
    w_out16 = w_out.astype(BF16)
    lam_vecs = jnp.stack([lq1, lk1, lq2, lk2]).astype(F32)
    g_col = g_subln.reshape(-1, 1)

    tabs_p = _rope_tables(jnp.arange(s, dtype=I32))
    kd, vd, km, vm, qm, qdT, qmT, kdh, kmh, vdT, vmT = _qkv(xp, mod_p[1], mod_p[0], g_attn_pre, w_in16, tabs_p, True)
    sel = _moba_select(qm, km)
    od = _flash(qdT, kdh, vdT, (lam_vecs, g_col), True, lam_init)
    om = _flash(qmT, kmh, vmT, (sel,), False, lam_init)
    x1_p, h2_p, sT_p = _post_attn(od, om, xp, mod_p[2], mod_p[4], mod_p[3], g_attn_post, g_ffn_pre, w_out16, w_router.T)

    tabs_s = _rope_tables(jnp.full((nseq,), past, dtype=I32))
    qd_s, kd_s, vd_s, qm_s, km_s, vm_s = _qkv(xs.reshape(1, nseq, d), mod_s[1], mod_s[0], g_attn_pre, w_in16, tabs_s, False)
    flat = lambda c: c.reshape(c.shape[0], c.shape[1], -1)
    od_s = _dec_diff(page_table, qd_s[0], kd_s[0], vd_s[0], flat(ckd), flat(cvd), lam_vecs, g_subln, lam_init)
    sel_s = _dec_gate(page_table, qm_s[0], flat(ckm))
    om_s = _dec_moba(page_table, sel_s[:, :, :MOBA_TOPK].reshape(nseq, -1), qm_s[0], km_s[0], vm_s[0], flat(ckm), flat(cvm))
    x1_s, h2_s, sT_s = _post_attn(od_s.reshape(1, nseq, w), om_s.reshape(1, nseq, w), xs.reshape(1, nseq, d),
                                  mod_s[2], mod_s[4], mod_s[3], g_attn_post, g_ffn_pre, w_out16, w_router.T)

    n_all = b * s + nseq
    h2 = jnp.concatenate([h2_p.reshape(b * s, d), h2_s.reshape(nseq, d)], axis=0)
    idx, pos, wts, cnt = _route(jnp.concatenate([sT_p, sT_s], axis=1), b_router)
    ne = w_router.shape[1]
    counts = cnt[:, 0].astype(I32)
    padded = (counts + MOE_BLOCK - 1) // MOE_BLOCK * MOE_BLOCK
    pad_end = jnp.cumsum(padded)
    pad_start = pad_end - padded
    n_blk = n_all * TOP_K // MOE_BLOCK + ne
    blk_e = jnp.minimum(jnp.searchsorted(pad_end, jnp.arange(n_blk, dtype=I32) * MOE_BLOCK, side='right'), ne - 1).astype(I32)
    dest = (pad_start[idx] + pos).astype(I32)
    xs_sorted = _dispatch(dest, h2, n_blk * MOE_BLOCK)
    wgu16 = jnp.concatenate([w_exp_gate, w_exp_up], axis=-1).astype(BF16)
    ys = _experts(blk_e, xs_sorted, wgu16, w_exp_down.astype(BF16))
    wsgu16 = jnp.concatenate([w_sh_gate, w_sh_up], axis=-1).astype(BF16)
    wsd16 = w_sh_down.astype(BF16)
    wts_t = wts.T
    y_p = _combine(dest, ys, wts_t, h2, x1_p, mod_p[5], g_ffn_post, wsgu16, wsd16, 0)
    y_s = _combine(dest, ys, wts_t, h2, x1_s, mod_s[5], g_ffn_post, wsgu16, wsd16, b * s)

    hd = ckd.shape[2:]
    hm = ckm.shape[2:]
    rows_p = (kd.reshape((b, s) + hd), vd.reshape((b, s) + hd), km.reshape((b, s) + hm), vm.reshape((b, s) + hm))
    rows_s = (kd_s.reshape((nseq, 1) + hd), vd_s.reshape((nseq, 1) + hd), km_s.reshape((nseq, 1) + hm), vm_s.reshape((nseq, 1) + hm))
    return y_p, y_s.reshape(nseq, 1, d), rows_p, rows_s


def kernel(x_prompt, x_sample, c_prompt, c_sample, cache_k_diff, cache_v_diff, cache_k_moba, cache_v_moba, page_table, w_ada, b_ada, g_attn_pre, g_attn_post, g_ffn_pre, g_ffn_post, w_in, w_out, lambda_q1, lambda_k1, lambda_q2, lambda_k2, g_subln, w_router, b_router, w_exp_gate, w_exp_up, w_exp_down, w_sh_gate, w_sh_up, w_sh_down):
    import math
    depth = w_ada.shape[0]
    yp, ys = x_prompt, x_sample
    rows_p, rows_s = [], []
    for l in range(depth):
        lam_init = 0.8 - 0.6 * math.exp(-0.3 * l)
        yp, ys, rp, rs = _layer(
            yp, ys, c_prompt, c_sample, cache_k_diff[l], cache_v_diff[l], cache_k_moba[l], cache_v_moba[l], page_table,
            lam_init, w_ada[l], b_ada[l], g_attn_pre[l], g_attn_post[l], g_ffn_pre[l], g_ffn_post[l], w_in[l], w_out[l],
            lambda_q1[l], lambda_k1[l], lambda_q2[l], lambda_k2[l], g_subln[l], w_router[l], b_router[l],
            w_exp_gate[l], w_exp_up[l], w_exp_down[l], w_sh_gate[l], w_sh_up[l], w_sh_down[l])
        rows_p.append(rp)
        rows_s.append(rs)
    stack = lambda rows, i: jnp.stack([r[i] for r in rows])
    return (yp, ys, stack(rows_p, 0), stack(rows_p, 1), stack(rows_p, 2), stack(rows_p, 3),
            stack(rows_s, 0), stack(rows_s, 1), stack(rows_s, 2), stack(rows_s, 3))
```

```python
import functools

import jax
import jax.numpy as jnp
from jax import lax
from jax.experimental import pallas as pl
from jax.experimental.pallas import tpu as pltpu

F32 = jnp.float32
BF16 = jnp.bfloat16
I32 = jnp.int32

HEAD_DIM = 64
ROT_DIMS = 16
ROPE_THETA = 500000.0
MOBA_BLOCK = 256
MOBA_TOPK = 3
N_GROUPS = 8
TOPK_GROUPS = 4
TOP_K = 8
ROUTED_SCALE = 2.5
MOE_BLOCK = 256
EPS = 1e-6
NEG = -1e30
LOG2E = 1.4426950408889634

LANES = 128
VMEM_LIMIT = 48 * 1024 * 1024


def _cparams(sem, vmem=VMEM_LIMIT):
    return pltpu.CompilerParams(dimension_semantics=sem, vmem_limit_bytes=vmem)


def _pick(n, candidates):
    for c in candidates:
        if c <= n and n % c == 0:
            return c
    return n


def _split_bf16(a):
    hi = a.astype(BF16)
    lo = (a - hi.astype(F32)).astype(BF16)
    return hi, lo


def _dot3(a, b, dims):
    ah, al = _split_bf16(a)
    bh, bl = _split_bf16(b)
    d = lambda x, y: lax.dot_general(x, y, (dims, ((), ())), preferred_element_type=F32)
    return d(ah, bh) + d(ah, bl) + d(al, bh)


_NN = ((1,), (0,))
_NT = ((1,), (1,))


def _rms(x, g):
    return x * lax.rsqrt(jnp.mean(x * x, axis=-1, keepdims=True) + EPS) * g


def _sigmoid(x):
    return 1.0 / (1.0 + jnp.exp(-x))


def _mod_kernel(c_ref, w_ref, b_ref, o_ref):
    c = c_ref[...]
    s = c * _sigmoid(c)
    o_ref[...] = _dot3(s, w_ref[...], _NN) + b_ref[...]


def _adaln(c, w_ada, b_ada):
    n, d = c.shape
    e = w_ada.shape[1]
    tn = _pick(e, (1024, 512, 256, 128))
    return pl.pallas_call(
        _mod_kernel,
        grid=(e // tn,),
        in_specs=[pl.BlockSpec((n, d), lambda j: (0, 0)),
                  pl.BlockSpec((d, tn), lambda j: (0, j)),
                  pl.BlockSpec((1, tn), lambda j: (0, j))],
        out_specs=pl.BlockSpec((n, tn), lambda j: (0, j)),
        out_shape=jax.ShapeDtypeStruct((n, e), F32),
        compiler_params=_cparams(("arbitrary",)),
    )(c, w_ada, b_ada.reshape(1, e))


def _rope_tables(pos):
    half = ROT_DIMS // 2
    inv_freq = ROPE_THETA ** (-2.0 * jnp.arange(half, dtype=F32) / ROT_DIMS)
    ang = pos.astype(F32)[:, None] * inv_freq[None, :]
    cos, sin = jnp.cos(ang), jnp.sin(ang)
    lane = jnp.arange(LANES) % HEAD_DIM
    idx = lane % half
    c = jnp.where(lane < ROT_DIMS, cos[:, idx], 1.0)
    sa = jnp.where(lane < half, -sin[:, idx], 0.0)
    sb = jnp.where((lane >= half) & (lane < ROT_DIMS), sin[:, idx], 0.0)
    return c.astype(F32), sa.astype(F32), sb.astype(F32)


def _rope(a, c, sa, sb):
    half = ROT_DIMS // 2
    out = []
    for j in range(a.shape[1] // LANES):
        x = a[:, j * LANES:(j + 1) * LANES]
        out.append(x * c + pltpu.roll(x, LANES - half, 1) * sa + pltpu.roll(x, half, 1) * sb)
    return jnp.concatenate(out, axis=1)


def _qkv_kernel(x_ref, sc_ref, sh_ref, g_ref, w_ref, c_ref, sa_ref, sb_ref, *outs, width, attn_layouts):
    x = x_ref[0]
    h = _rms(x, g_ref[...]) * (1.0 + sc_ref[0]) + sh_ref[0]
    proj = jnp.dot(h.astype(BF16), w_ref[...], preferred_element_type=F32)
    c, sa, sb = c_ref[...], sa_ref[...], sb_ref[...]
    w = width
    scale = HEAD_DIM ** -0.5 * (LOG2E if attn_layouts else 1.0)
    qd = _rope(proj[:, 0 * w:1 * w], c, sa, sb) * scale
    kd = _rope(proj[:, 1 * w:2 * w], c, sa, sb)
    vd = proj[:, 2 * w:3 * w]
    qm = _rope(proj[:, 3 * w:4 * w], c, sa, sb) * scale
    km = _rope(proj[:, 4 * w:5 * w], c, sa, sb)
    vm = proj[:, 5 * w:6 * w]
    if not attn_layouts:
        for r, v in zip(outs, (qd, kd, vd, qm, km, vm)):
            r[0] = v
        return
    kd_o, vd_o, km_o, vm_o, qm_o, qdT_o, qmT_o, kdh_o, kmh_o, vdT_o, vmT_o = outs
    kd_o[0] = kd
    vd_o[0] = vd
    km_o[0] = km
    vm_o[0] = vm
    qm_o[0] = qm
    qdT_o[0] = qd.T.astype(BF16)
    qmT_o[0] = qm.T.astype(BF16)
    for m in range(w // HEAD_DIM):
        kdh_o[0, m] = kd[:, m * HEAD_DIM:(m + 1) * HEAD_DIM].astype(BF16)
        kmh_o[0, m] = km[:, m * HEAD_DIM:(m + 1) * HEAD_DIM].astype(BF16)
    for r in range(x.shape[0] // MOBA_BLOCK):
        vdT_o[0, r] = vd[r * MOBA_BLOCK:(r + 1) * MOBA_BLOCK].T.astype(BF16)
        vmT_o[0, r] = vm[r * MOBA_BLOCK:(r + 1) * MOBA_BLOCK].T.astype(BF16)


def _qkv(x, sc, sh, g, w_in16, tabs, attn_layouts):
    b, s, d = x.shape
    w = w_in16.shape[1] // 6
    ts = _pick(s, (512, 256)) if attn_layouts else s
    sm = sc.shape[1]
    mod_spec = pl.BlockSpec((1, 1 if sm == 1 else ts, d), (lambda i, j: (i, 0, 0)) if sm == 1 else (lambda i, j: (i, j, 0)))
    tab_spec = pl.BlockSpec((ts, LANES), lambda i, j: (j, 0))
    row_spec = pl.BlockSpec((1, ts, w), lambda i, j: (i, j, 0))
    row_shape = jax.ShapeDtypeStruct((b, s, w), F32)
    if attn_layouts:
        nb = s // MOBA_BLOCK
        nh = w // HEAD_DIM
        out_specs = [row_spec] * 5 + [
            pl.BlockSpec((1, w, ts), lambda i, j: (i, 0, j)),
            pl.BlockSpec((1, w, ts), lambda i, j: (i, 0, j)),
            pl.BlockSpec((1, nh, ts, HEAD_DIM), lambda i, j: (i, 0, j, 0)),
            pl.BlockSpec((1, nh, ts, HEAD_DIM), lambda i, j: (i, 0, j, 0)),
            pl.BlockSpec((1, ts // MOBA_BLOCK, w, MOBA_BLOCK), lambda i, j: (i, j, 0, 0)),
            pl.BlockSpec((1, ts // MOBA_BLOCK, w, MOBA_BLOCK), lambda i, j: (i, j, 0, 0)),
        ]
        out_shape = [row_shape] * 5 + [
            jax.ShapeDtypeStruct((b, w, s), BF16), jax.ShapeDtypeStruct((b, w, s), BF16),
            jax.ShapeDtypeStruct((b, nh, s, HEAD_DIM), BF16), jax.ShapeDtypeStruct((b, nh, s, HEAD_DIM), BF16),
            jax.ShapeDtypeStruct((b, nb, w, MOBA_BLOCK), BF16), jax.ShapeDtypeStruct((b, nb, w, MOBA_BLOCK), BF16),
        ]
    else:
        out_specs = [row_spec] * 6
        out_shape = [row_shape] * 6
    return pl.pallas_call(
        functools.partial(_qkv_kernel, width=w, attn_layouts=attn_layouts),
        grid=(b, s // ts),
        in_specs=[pl.BlockSpec((1, ts, d), lambda i, j: (i, j, 0)), mod_spec, mod_spec,
                  pl.BlockSpec((1, d), lambda i, j: (0, 0)),
                  pl.BlockSpec(w_in16.shape, lambda i, j: (0, 0)),
                  tab_spec, tab_spec, tab_spec],
        out_specs=out_specs,
        out_shape=out_shape,
        compiler_params=_cparams(("arbitrary", "arbitrary")),
    )(x, sc, sh, g.reshape(1, d), w_in16, *tabs)


def _select_kernel(q_ref, k_ref, o_ref, kmean_ref, *, nb, nh):
    j = pl.program_id(1)

    @pl.when(j == 0)
    def _():
        kmean_ref[...] = jnp.zeros_like(kmean_ref)

    kmean_ref[pl.ds(j, 1), :] = jnp.mean(k_ref[0], axis=0, keepdims=True)
    kmean = kmean_ref[...]
    lane_head = lax.broadcasted_iota(I32, kmean.shape, 1) // HEAD_DIM
    kbd = jnp.concatenate([jnp.where(lane_head == h, kmean, 0.0) for h in range(nh)], axis=0)
    gate = _dot3(kbd, q_ref[0], _NT).reshape(nh, nb, -1)
    row = lax.broadcasted_iota(I32, gate.shape, 1)
    rank = jnp.zeros(gate.shape, F32)
    for jp in range(nb):
        other = gate[:, jp:jp + 1, :]
        beats = (other > gate) | ((other == gate) & (jp < row))
        rank = rank + jnp.where(beats & (jp < j), 1.0, 0.0)
    o_ref[0] = jnp.where((row < j) & (rank < MOBA_TOPK), 1.0, 0.0)


def _moba_select(qm, km):
    b, s, w = qm.shape
    nb = s // MOBA_BLOCK
    nh = w // HEAD_DIM
    return pl.pallas_call(
        functools.partial(_select_kernel, nb=nb, nh=nh),
        grid=(b, nb),
        in_specs=[pl.BlockSpec((1, MOBA_BLOCK, w), lambda i, j: (i, j, 0)),
                  pl.BlockSpec((1, MOBA_BLOCK, w), lambda i, j: (i, j, 0))],
        out_specs=pl.BlockSpec((1, nh, nb, MOBA_BLOCK), lambda i, j: (i, 0, 0, j)),
        out_shape=jax.ShapeDtypeStruct((b, nh, nb, s), F32),
        scratch_shapes=[pltpu.VMEM((nb, w), F32)],
        compiler_params=_cparams(("arbitrary", "arbitrary")),
    )(qm, km)


def _flash_kernel(*refs, diff, lam_init):
    if diff:
        q_ref, k_ref, v_ref, lam_ref, g_ref, o_ref = refs
        sel_ref = None
    else:
        q_ref, k_ref, v_ref, sel_ref, o_ref = refs
    i = pl.program_id(2)
    tq = q_ref.shape[2]
    tk = MOBA_BLOCK
    nqb = tq // tk
    dv = v_ref.shape[2] if diff else HEAD_DIM
    qs = (q_ref[0, 0:HEAD_DIM, :], q_ref[0, HEAD_DIM:2 * HEAD_DIM, :])

    def step(jj, carry, diagonal):
        new = []
        for s in range(2):
            m, l, acc = carry[s]
            sts = []
            for h in range(nqb):
                blk = jj * nqb + h
                k = k_ref[0, s, pl.ds(pl.multiple_of(blk * tk, tk), tk), :]
                st = jnp.dot(k, qs[s], preferred_element_type=F32)
                if diagonal:
                    key = lax.broadcasted_iota(I32, st.shape, 0) + h * tk
                    qry = lax.broadcasted_iota(I32, st.shape, 1)
                    keep = key <= qry
                    if sel_ref is not None and h < nqb - 1:
                        keep = keep & ((qry < (h + 1) * tk) | (sel_ref[0, s, pl.ds(blk, 1), :] > 0.0))
                    st = jnp.where(keep, st, NEG)
                elif sel_ref is not None:
                    st = jnp.where(sel_ref[0, s, pl.ds(blk, 1), :] > 0.0, st, NEG)
                sts.append(st)
            m_new = m
            for st in sts:
                m_new = jnp.maximum(m_new, jnp.max(st, axis=0, keepdims=True))
            alpha = jnp.exp2(m - m_new)
            l = alpha * l
            acc = alpha * acc
            for h, st in enumerate(sts):
                blk = jj * nqb + h
                p = jnp.exp2(st - m_new)
                l = l + jnp.sum(p, axis=0, keepdims=True)
                v = v_ref[0, blk] if diff else v_ref[0, blk, s * HEAD_DIM:(s + 1) * HEAD_DIM, :]
                acc = acc + jnp.dot(v, p.astype(BF16), preferred_element_type=F32)
            new.append((m_new, l, acc))
        return tuple(new)

    init = tuple((jnp.full((1, tq), NEG, F32), jnp.zeros((1, tq), F32), jnp.zeros((dv, tq), F32)) for _ in range(2))
    carry = lax.fori_loop(0, i, lambda jj, c: step(jj, c, False), init)
    (m0, l0, a0), (m1, l1, a1) = step(i, carry, True)
    o0 = a0 / l0
    o1 = a1 / l1
    if diff:
        lv = lam_ref[...]
        lam = (jnp.exp(jnp.sum(lv[0:1] * lv[1:2], axis=1, keepdims=True))
               - jnp.exp(jnp.sum(lv[2:3] * lv[3:4], axis=1, keepdims=True)) + lam_init)
        o = o0 - lam * o1
        o = o * lax.rsqrt(jnp.mean(o * o, axis=0, keepdims=True) + EPS) * g_ref[...] * (1.0 - lam_init)
    else:
        o = jnp.concatenate([o0, o1], axis=0)
    o_ref[0] = o.T.astype(o_ref.dtype)


FLASH_TQ = 1024


def _flash(qT, k_hm, vT, extra, diff, lam_init):
    b, w, s = qT.shape
    nb = s // MOBA_BLOCK
    tq = FLASH_TQ if s % FLASH_TQ == 0 else MOBA_BLOCK
    ng = w // (2 * HEAD_DIM)
    in_specs = [pl.BlockSpec((1, 2 * HEAD_DIM, tq), lambda bi, g, i: (bi, g, i)),
                pl.BlockSpec((1, 2, s, HEAD_DIM), lambda bi, g, i: (bi, g, 0, 0)),
                pl.BlockSpec((1, nb, 2 * HEAD_DIM, MOBA_BLOCK), lambda bi, g, i: (bi, 0, g, 0))]
    if diff:
        lam_vecs, g_sub = extra
        in_specs += [pl.BlockSpec(lam_vecs.shape, lambda bi, g, i: (0, 0)),
                     pl.BlockSpec(g_sub.shape, lambda bi, g, i: (0, 0))]
        args = (lam_vecs, g_sub)
    else:
        (sel,) = extra
        in_specs += [pl.BlockSpec((1, 2, nb, tq), lambda bi, g, i: (bi, g, 0, i))]
        args = (sel,)
    return pl.pallas_call(
        functools.partial(_flash_kernel, diff=diff, lam_init=lam_init),
        grid=(b, ng, s // tq),
        in_specs=in_specs,
        out_specs=pl.BlockSpec((1, tq, 2 * HEAD_DIM), lambda bi, g, i: (bi, i, g)),
        out_shape=jax.ShapeDtypeStruct((b, s, w), BF16),
        compiler_params=_cparams(("arbitrary", "arbitrary", "arbitrary")),
    )(qT, k_hm, vT, *args)


def _post_kernel(od_ref, om_ref, x_ref, gt_ref, sc_ref, sh_ref, ga_ref, gf_ref, w_ref, wr_ref,
                 x1_ref, h2_ref, sc_out_ref):
    w = od_ref.shape[2]
    y = (jnp.dot(od_ref[0].astype(BF16), w_ref[0:w, :], preferred_element_type=F32)
         + jnp.dot(om_ref[0].astype(BF16), w_ref[w:2 * w, :], preferred_element_type=F32))
    x1 = x_ref[0] + gt_ref[0] * _rms(y, ga_ref[...])
    h2 = _rms(x1, gf_ref[...]) * (1.0 + sc_ref[0]) + sh_ref[0]
    x1_ref[0] = x1
    h2_ref[0] = h2
    sc_out_ref[...] = _sigmoid(_dot3(wr_ref[...], h2, _NT))


def _post_attn(od, om, x, gt, sc, sh, g_post, g_pre, w_out16, w_rT):
    b, s, d = x.shape
    w = od.shape[2]
    ne = w_rT.shape[0]
    ts = _pick(s, (512, 256, 128))
    sm = gt.shape[1]
    mod_spec = pl.BlockSpec((1, 1 if sm == 1 else ts, d), (lambda i, j: (i, 0, 0)) if sm == 1 else (lambda i, j: (i, j, 0)))
    vec_spec = pl.BlockSpec((1, d), lambda i, j: (0, 0))
    tok_spec = pl.BlockSpec((1, ts, d), lambda i, j: (i, j, 0))
    ns = s // ts
    return pl.pallas_call(
        _post_kernel,
        grid=(b, ns),
        in_specs=[pl.BlockSpec((1, ts, w), lambda i, j: (i, j, 0)), pl.BlockSpec((1, ts, w), lambda i, j: (i, j, 0)),
                  tok_spec, mod_spec, mod_spec, mod_spec, vec_spec, vec_spec,
                  pl.BlockSpec(w_out16.shape, lambda i, j: (0, 0)),
                  pl.BlockSpec(w_rT.shape, lambda i, j: (0, 0))],
        out_specs=[tok_spec, tok_spec, pl.BlockSpec((ne, ts), lambda i, j: (0, i * ns + j))],
        out_shape=[jax.ShapeDtypeStruct((b, s, d), F32), jax.ShapeDtypeStruct((b, s, d), F32),
                   jax.ShapeDtypeStruct((ne, b * s), F32)],
        compiler_params=_cparams(("arbitrary", "arbitrary")),
    )(od, om, x, gt, sc, sh, g_post.reshape(1, d), g_pre.reshape(1, d), w_out16, w_rT)


def _take_max(x, iota, fill):
    m = jnp.max(x, axis=0, keepdims=True)
    idx = jnp.min(jnp.where(x == m, iota, fill), axis=0, keepdims=True)
    hot = iota == idx
    return m, idx, hot, jnp.where(hot, -jnp.inf, x)


def _route_kernel(s_ref, b_ref, tri_ref, idx_ref, pos_ref, w_ref, cnt_ref, run_ref):
    step = pl.program_id(0)

    @pl.when(step == 0)
    def _():
        run_ref[...] = jnp.zeros_like(run_ref)

    scores = s_ref[...]
    ne, tn = scores.shape
    gs = ne // N_GROUPS
    biased = scores + b_ref[...]
    grp = biased.reshape(N_GROUPS, gs, tn)
    sub = lax.broadcasted_iota(I32, grp.shape, 1)
    m1 = jnp.max(grp, axis=1, keepdims=True)
    i1 = jnp.min(jnp.where(grp == m1, sub, gs), axis=1, keepdims=True)
    m2 = jnp.max(jnp.where(sub == i1, -jnp.inf, grp), axis=1, keepdims=True)
    gscore = (m1 + m2).reshape(N_GROUPS, tn)
    giota = lax.broadcasted_iota(I32, gscore.shape, 0)
    gmask = jnp.zeros(gscore.shape, jnp.bool_)
    for _ in range(TOPK_GROUPS):
        _, _, hot, gscore = _take_max(gscore, giota, N_GROUPS)
        gmask = gmask | hot
    emask = jnp.broadcast_to(gmask.reshape(N_GROUPS, 1, tn), (N_GROUPS, gs, tn)).reshape(ne, tn)
    cand = jnp.where(emask, biased, -jnp.inf)
    eiota = lax.broadcasted_iota(I32, cand.shape, 0)
    chosen = jnp.zeros(cand.shape, F32)
    idxs, wts = [], []
    for _ in range(TOP_K):
        _, idx, hot, cand = _take_max(cand, eiota, ne)
        idxs.append(idx)
        wts.append(jnp.sum(jnp.where(hot, scores, 0.0), axis=0, keepdims=True))
        chosen = chosen + jnp.where(hot, 1.0, 0.0)
    wsum = wts[0]
    for t in wts[1:]:
        wsum = wsum + t
    before = run_ref[...] + jnp.dot(chosen.astype(BF16), tri_ref[...], preferred_element_type=F32)
    poss = [jnp.sum(jnp.where(eiota == idx, before, 0.0), axis=0, keepdims=True) for idx in idxs]
    idx_ref[...] = jnp.concatenate(idxs, axis=0)
    pos_ref[...] = jnp.concatenate(poss, axis=0).astype(I32)
    w_ref[...] = jnp.concatenate(wts, axis=0) / wsum * ROUTED_SCALE
    run = run_ref[...] + jnp.sum(chosen, axis=1, keepdims=True)
    run_ref[...] = run
    cnt_ref[...] = jnp.broadcast_to(run, cnt_ref.shape)


def _route(scoresT, b_router):
    ne, n = scoresT.shape
    tn = _pick(n, (1152, 1024, 512, 384, 256, 128))
    tri = jnp.triu(jnp.ones((tn, tn), BF16), k=1)
    blk = pl.BlockSpec((TOP_K, tn), lambda i: (0, i))
    return pl.pallas_call(
        _route_kernel,
        grid=(n // tn,),
        in_specs=[pl.BlockSpec((ne, tn), lambda i: (0, i)),
                  pl.BlockSpec((ne, 1), lambda i: (0, 0)),
                  pl.BlockSpec((tn, tn), lambda i: (0, 0))],
        out_specs=[blk, blk, blk, pl.BlockSpec((ne, LANES), lambda i: (0, 0))],
        out_shape=[jax.ShapeDtypeStruct((TOP_K, n), I32), jax.ShapeDtypeStruct((TOP_K, n), I32),
                   jax.ShapeDtypeStruct((TOP_K, n), F32), jax.ShapeDtypeStruct((ne, LANES), F32)],
        scratch_shapes=[pltpu.VMEM((ne, 1), F32)],
        compiler_params=_cparams(("arbitrary",)),
    )(scoresT, b_router.reshape(ne, 1), tri)


def _row_copy(src, si, dst, di, sem):
    return pltpu.make_async_copy(src.at[pl.ds(si, 1)], dst.at[pl.ds(di, 1)], sem)


def _dispatch_kernel(dest_ref, h_ref, xs_ref, sem):
    tn = h_ref.shape[0]

    def issue(t, _):
        for k in range(TOP_K):
            _row_copy(h_ref, t, xs_ref, dest_ref[k, t], sem).start()
        return _

    lax.fori_loop(0, tn, issue, 0)

    def drain(t, _):
        for k in range(TOP_K):
            _row_copy(h_ref, t, xs_ref, dest_ref[k, t], sem).wait()
        return _

    lax.fori_loop(0, tn, drain, 0)


def _dispatch(dest, h2, n_rows):
    n, d = h2.shape
    tn = _pick(n, (128,))
    return pl.pallas_call(
        _dispatch_kernel,
        grid=(n // tn,),
        in_specs=[pl.BlockSpec((TOP_K, tn), lambda i: (0, i), memory_space=pltpu.SMEM),
                  pl.BlockSpec((tn, d), lambda i: (i, 0))],
        out_specs=pl.BlockSpec(memory_space=pl.ANY),
        out_shape=jax.ShapeDtypeStruct((n_rows, d), F32),
        scratch_shapes=[pltpu.SemaphoreType.DMA],
        compiler_params=_cparams(("arbitrary",)),
    )(dest, h2)


def _expert_kernel(be_ref, nv_ref, x_ref, wgu_ref, wd_ref, o_ref):
    del be_ref
    de = wd_ref.shape[1]
    x = x_ref[...]
    x = jnp.where(lax.broadcasted_iota(I32, x.shape, 0) < nv_ref[pl.program_id(0)], x, 0.0)
    gu = jnp.dot(x.astype(BF16), wgu_ref[0], preferred_element_type=F32)
    g, u = gu[:, :de], gu[:, de:]
    mid = g * _sigmoid(g) * u
    o_ref[...] = jnp.dot(mid.astype(BF16), wd_ref[0], preferred_element_type=F32)


def _experts(blk_e, blk_valid, xs, wgu16, wd16):
    n_rows, d = xs.shape
    de = wd16.shape[1]
    return pl.pallas_call(
        _expert_kernel,
        grid_spec=pltpu.PrefetchScalarGridSpec(
            num_scalar_prefetch=2,
            grid=(n_rows // MOE_BLOCK,),
            in_specs=[pl.BlockSpec((MOE_BLOCK, d), lambda i, be, nv: (i, 0)),
                      pl.BlockSpec((1, d, 2 * de), lambda i, be, nv: (be[i], 0, 0)),
                      pl.BlockSpec((1, de, d), lambda i, be, nv: (be[i], 0, 0))],
            out_specs=pl.BlockSpec((MOE_BLOCK, d), lambda i, be, nv: (i, 0))),
        out_shape=jax.ShapeDtypeStruct((n_rows, d), F32),
        compiler_params=_cparams(("arbitrary",)),
    )(blk_e, blk_valid, xs, wgu16, wd16)


def _combine_kernel(dest_ref, ys_ref, w_ref, h_ref, x1_ref, gt_ref, g_ref, wsgu_ref, wsd_ref, o_ref, rows_ref, sem):
    tn = h_ref.shape[0]

    def issue(t, _):
        for k in range(TOP_K):
            _row_copy(ys_ref, dest_ref[k, t], rows_ref.at[k], t, sem).start()
        return _

    lax.fori_loop(0, tn, issue, 0)
    de = wsd_ref.shape[0]
    gu = jnp.dot(h_ref[...].astype(BF16), wsgu_ref[...], preferred_element_type=F32)
    g, u = gu[:, :de], gu[:, de:]
    f = jnp.dot((g * _sigmoid(g) * u).astype(BF16), wsd_ref[...], preferred_element_type=F32)

    def drain(t, _):
        for k in range(TOP_K):
            _row_copy(ys_ref, dest_ref[k, t], rows_ref.at[k], t, sem).wait()
        return _

    lax.fori_loop(0, tn, drain, 0)
    w = w_ref[...]
    for k in range(TOP_K):
        f = f + rows_ref[k] * w[:, k:k + 1]
    o_ref[0] = x1_ref[0] + gt_ref[0] * _rms(f, g_ref[...])


def _combine(dest, ys, wts, h2, x1, gt, g_post, wsgu16, wsd16, tok0):
    b, s, d = x1.shape
    tn = _pick(s, (128,))
    ns = s // tn
    off = tok0 // tn
    sm = gt.shape[1]
    mod_spec = pl.BlockSpec((1, 1 if sm == 1 else tn, d), (lambda i, j: (i, 0, 0)) if sm == 1 else (lambda i, j: (i, j, 0)))
    return pl.pallas_call(
        _combine_kernel,
        grid=(b, ns),
        in_specs=[pl.BlockSpec((TOP_K, tn), lambda i, j: (0, off + i * ns + j), memory_space=pltpu.SMEM),
                  pl.BlockSpec(memory_space=pl.ANY),
                  pl.BlockSpec((tn, TOP_K), lambda i, j: (off + i * ns + j, 0)),
                  pl.BlockSpec((tn, d), lambda i, j: (off + i * ns + j, 0)),
                  pl.BlockSpec((1, tn, d), lambda i, j: (i, j, 0)),
                  mod_spec,
                  pl.BlockSpec((1, d), lambda i, j: (0, 0)),
                  pl.BlockSpec(wsgu16.shape, lambda i, j: (0, 0)),
                  pl.BlockSpec(wsd16.shape, lambda i, j: (0, 0))],
        out_specs=pl.BlockSpec((1, tn, d), lambda i, j: (i, j, 0)),
        out_shape=jax.ShapeDtypeStruct((b, s, d), F32),
        scratch_shapes=[pltpu.VMEM((TOP_K, tn, d), F32), pltpu.SemaphoreType.DMA],
        compiler_params=_cparams(("arbitrary", "arbitrary")),
    )(dest, ys, wts, h2, x1, gt, g_post.reshape(1, d), wsgu16, wsd16)


PAGES_PER_STEP = 8


def _per_map(x, nmaps):
    row_head = lax.broadcasted_iota(I32, (nmaps, x.shape[1]), 0) // 2
    out = jnp.zeros((nmaps, x.shape[1]), x.dtype)
    for h in range(x.shape[0]):
        out = jnp.where(row_head == h, x[h:h + 1, :], out)
    return out


def _dec_diff_kernel(pt_ref, q_ref, kn_ref, vn_ref, lam_ref, g_ref, *refs, lam_init, nheads):
    del pt_ref
    k_refs = refs[:PAGES_PER_STEP]
    v_refs = refs[PAGES_PER_STEP:2 * PAGES_PER_STEP]
    o_ref, m_ref, l_ref, acc_ref = refs[2 * PAGES_PER_STEP:]
    p = pl.program_id(1)
    nmaps = 2 * nheads
    dv = q_ref.shape[2]
    lane_map = lax.broadcasted_iota(I32, (nmaps, dv), 1) // HEAD_DIM
    row_map = lax.broadcasted_iota(I32, (nmaps, dv), 0) % 2
    q8 = jnp.where(lane_map == row_map, _per_map(q_ref[0], nmaps), 0.0)
    q16 = q8.astype(BF16)

    @pl.when(p == 0)
    def _():
        m_ref[...] = jnp.sum(q8 * _per_map(kn_ref[0], nmaps), axis=1, keepdims=True)
        l_ref[...] = jnp.ones_like(l_ref)
        acc_ref[...] = _per_map(vn_ref[0], nmaps)

    rows = k_refs[0].shape[1]
    own = (lax.broadcasted_iota(I32, (nmaps, rows), 1) % nheads) == (lax.broadcasted_iota(I32, (nmaps, rows), 0) // 2)
    st = jnp.concatenate(
        [jnp.where(own, lax.dot_general(q16, k_refs[r][0].astype(BF16), (_NT, ((), ())), preferred_element_type=F32), NEG)
         for r in range(PAGES_PER_STEP)], axis=1)
    m_old = m_ref[...]
    m_new = jnp.maximum(m_old, jnp.max(st, axis=1, keepdims=True))
    alpha = jnp.exp(m_old - m_new)
    pr = jnp.exp(st - m_new)
    l_ref[...] = alpha * l_ref[...] + jnp.sum(pr, axis=1, keepdims=True)
    pr16 = pr.astype(BF16)
    pv = jnp.zeros(acc_ref.shape, F32)
    for r in range(PAGES_PER_STEP):
        pv = pv + jnp.dot(pr16[:, r * rows:(r + 1) * rows], v_refs[r][0].astype(BF16), preferred_element_type=F32)
    acc_ref[...] = alpha * acc_ref[...] + pv
    m_ref[...] = m_new

    @pl.when(p == pl.num_programs(1) - 1)
    def _():
        o = acc_ref[...] / l_ref[...]
        lv = lam_ref[...]
        lam = (jnp.exp(jnp.sum(lv[0:1] * lv[1:2], axis=1, keepdims=True))
               - jnp.exp(jnp.sum(lv[2:3] * lv[3:4], axis=1, keepdims=True)) + lam_init)
        outs = [_rms(o[2 * h:2 * h + 1] - lam * o[2 * h + 1:2 * h + 2], g_ref[...]) * (1.0 - lam_init) for h in range(nheads)]
        o_ref[0] = jnp.concatenate(outs, axis=0)


def _dec_diff(page_table, q, k_new, v_new, cache_k, cache_v, lam_vecs, g_sub, lam_init):
    nseq, nheads, dv = q.shape
    npages = page_table.shape[1]
    rows = cache_k.shape[1]
    nsteps = npages // PAGES_PER_STEP

    def page_spec(r):
        return pl.BlockSpec((1, rows, dv), lambda b, p, pt: (pt[b, p * PAGES_PER_STEP + r], 0, 0))

    seq_spec = pl.BlockSpec((1, nheads, dv), lambda b, p, pt: (b, 0, 0))
    return pl.pallas_call(
        functools.partial(_dec_diff_kernel, lam_init=lam_init, nheads=nheads),
        grid_spec=pltpu.PrefetchScalarGridSpec(
            num_scalar_prefetch=1,
            grid=(nseq, nsteps),
            in_specs=[seq_spec, seq_spec, seq_spec,
                      pl.BlockSpec(lam_vecs.shape, lambda b, p, pt: (0, 0)),
                      pl.BlockSpec((1, dv), lambda b, p, pt: (0, 0))]
                     + [page_spec(r) for r in range(PAGES_PER_STEP)] * 2,
            out_specs=seq_spec,
            scratch_shapes=[pltpu.VMEM((2 * nheads, 1), F32), pltpu.VMEM((2 * nheads, 1), F32),
                            pltpu.VMEM((2 * nheads, dv), F32)]),
        out_shape=jax.ShapeDtypeStruct((nseq, nheads, dv), F32),
        compiler_params=_cparams(("arbitrary", "arbitrary")),
    )(page_table, q, k_new, v_new, lam_vecs, g_sub.reshape(1, dv),
      *([cache_k] * PAGES_PER_STEP), *([cache_v] * PAGES_PER_STEP))


def _dec_gate_kernel(pt_ref, q_ref, *refs, nh, pages_per_block, nblocks):
    del pt_ref
    k_refs = refs[:PAGES_PER_STEP]
    o_ref, gate_ref = refs[PAGES_PER_STEP:]
    p = pl.program_id(1)

    @pl.when(p == 0)
    def _():
        gate_ref[...] = jnp.zeros_like(gate_ref)

    w = q_ref.shape[2]
    lane_head = lax.broadcasted_iota(I32, (nh, w), 1) // HEAD_DIM
    row_head = lax.broadcasted_iota(I32, (nh, w), 0)
    qbd = jnp.where(lane_head == row_head, jnp.broadcast_to(q_ref[0], (nh, w)), 0.0)
    qh, ql = _split_bf16(qbd)
    lane = lax.broadcasted_iota(I32, (nh, LANES), 1)
    g = gate_ref[...]
    for r in range(PAGES_PER_STEP):
        kh, kl = _split_bf16(k_refs[r][0])
        s = (jnp.dot(qh, kh, preferred_element_type=F32) + jnp.dot(qh, kl, preferred_element_type=F32)
             + jnp.dot(ql, kh, preferred_element_type=F32))
        blk = (p * PAGES_PER_STEP + r) // pages_per_block
        g = g + jnp.where(lane == blk, jnp.sum(s, axis=1, keepdims=True), 0.0)
    gate_ref[...] = g

    @pl.when(p == pl.num_programs(1) - 1)
    def _():
        gate = jnp.where(lane < nblocks, g, -jnp.inf)
        out = jnp.zeros((nh, LANES), I32)
        for r in range(MOBA_TOPK):
            m = jnp.max(gate, axis=1, keepdims=True)
            idx = jnp.min(jnp.where(gate == m, lane, LANES), axis=1, keepdims=True)
            out = jnp.where(lane == r, idx, out)
            gate = jnp.where(lane == idx, -jnp.inf, gate)
        o_ref[0] = out


def _dec_gate(page_table, q, cache_kT):
    nseq, w = q.shape
    npages = page_table.shape[1]
    page = cache_kT.shape[2]
    nh = w // HEAD_DIM
    ppb = MOBA_BLOCK // page
    nblocks = npages // ppb
    assert nblocks <= LANES

    def page_spec(r):
        return pl.BlockSpec((1, w, page), lambda b, p, pt: (pt[b, p * PAGES_PER_STEP + r], 0, 0))

    return pl.pallas_call(
        functools.partial(_dec_gate_kernel, nh=nh, pages_per_block=ppb, nblocks=nblocks),
        grid_spec=pltpu.PrefetchScalarGridSpec(
            num_scalar_prefetch=1,
            grid=(nseq, npages // PAGES_PER_STEP),
            in_specs=[pl.BlockSpec((1, 1, w), lambda b, p, pt: (b, 0, 0))]
                     + [page_spec(r) for r in range(PAGES_PER_STEP)],
            out_specs=pl.BlockSpec((1, nh, LANES), lambda b, p, pt: (b, 0, 0)),
            scratch_shapes=[pltpu.VMEM((nh, LANES), F32)]),
        out_shape=jax.ShapeDtypeStruct((nseq, nh, LANES), I32),
        compiler_params=_cparams(("arbitrary", "arbitrary")),
    )(page_table, q.reshape(nseq, 1, w), *([cache_kT] * PAGES_PER_STEP))


def _dec_moba_kernel(pt_ref, sel_ref, q_ref, kn_ref, vn_ref, *refs, npg):
    del pt_ref, sel_ref
    k_refs = refs[:npg]
    v_refs = refs[npg:2 * npg]
    o_ref = refs[-1]
    q = q_ref[0, 0]
    q8 = jnp.where(lax.broadcasted_iota(I32, (8, HEAD_DIM), 0) == 0, jnp.broadcast_to(q, (8, HEAD_DIM)), 0.0)
    q16 = q8.astype(BF16)
    st = jnp.concatenate([jnp.dot(q16, k_refs[r][0].astype(BF16), preferred_element_type=F32) for r in range(npg)],
                         axis=1)
    s_new = jnp.sum(q8 * kn_ref[0, 0], axis=1, keepdims=True)
    m = jnp.maximum(jnp.max(st, axis=1, keepdims=True), s_new)
    pr = jnp.exp(st - m)
    p_new = jnp.exp(s_new - m)
    l = jnp.sum(pr, axis=1, keepdims=True) + p_new
    pr16 = pr.astype(BF16)
    page = k_refs[0].shape[2]
    acc = p_new * vn_ref[0, 0]
    for r in range(npg):
        acc = acc + lax.dot_general(pr16[:, r * page:(r + 1) * page], v_refs[r][0].astype(BF16), (_NT, ((), ())),
                                    preferred_element_type=F32)
    o_ref[0, 0] = (acc / l)[0:1]


def _dec_moba(page_table, sel, q, k_new, v_new, cache_kT, cache_vT):
    nseq, nh = q.shape[:2]
    page = cache_kT.shape[2]
    ppb = MOBA_BLOCK // page
    npg = MOBA_TOPK * ppb

    def page_spec(r):
        def imap(b, h, pt, sl):
            blk = sl[b, h * MOBA_TOPK + r // ppb]
            return (pt[b, blk * ppb + r % ppb], h, 0)
        return pl.BlockSpec((1, HEAD_DIM, page), imap)

    seq_spec = pl.BlockSpec((1, 1, 1, HEAD_DIM), lambda b, h, pt, sl: (b, h, 0, 0))
    return pl.pallas_call(
        functools.partial(_dec_moba_kernel, npg=npg),
        grid_spec=pltpu.PrefetchScalarGridSpec(
            num_scalar_prefetch=2,
            grid=(nseq, nh),
            in_specs=[seq_spec, seq_spec, seq_spec] + [page_spec(r) for r in range(npg)] * 2,
            out_specs=seq_spec),
        out_shape=jax.ShapeDtypeStruct((nseq, nh, 1, HEAD_DIM), F32),
        compiler_params=_cparams(("arbitrary", "arbitrary")),
    )(page_table, sel, q, k_new, v_new, *([cache_kT] * npg), *([cache_vT] * npg))


def _layer(xp, xs, cp, cs, ckd, cvd, ckm, cvm, page_table, lam_init, w_ada, b_ada, g_attn_pre, g_attn_post,
           g_ffn_pre, g_ffn_post, w_in, w_out, lq1, lk1, lq2, lk2, g_subln, w_router, b_router,
           w_exp_gate, w_exp_up, w_exp_down, w_sh_gate, w_sh_up, w_sh_down):
    b, s, d = xp.shape
    nseq = xs.shape[0]
    past = page_table.shape[1] * ckd.shape[1]
    w = w_in.shape[1] // 6
    assert s % MOBA_BLOCK == 0 and MOBA_BLOCK % ckd.shape[1] == 0
    assert page_table.shape[1] % PAGES_PER_STEP == 0 and past // MOBA_BLOCK >= MOBA_TOPK and past % MOBA_BLOCK == 0
    assert xs.shape[1] == 1 and nseq % LANES == 0 and (b * s) % LANES == 0

    mod = _adaln(jnp.concatenate([cp, cs], axis=0), w_ada, b_ada)
    mod_p = [m[:, None, :] for m in jnp.split(mod[:b], 6, axis=-1)]
    mod_s = [m[None, :, :] for m in jnp.split(mod[b:], 6, axis=-1)]

    w_in16 = w_in.astype(BF16)
    w_out16 = w_out.astype(BF16)
    lam_vecs = jnp.stack([lq1, lk1, lq2, lk2]).astype(F32)
    g_col = g_subln.reshape(-1, 1)

    tabs_p = _rope_tables(jnp.arange(s, dtype=I32))
    kd, vd, km, vm, qm, qdT, qmT, kdh, kmh, vdT, vmT = _qkv(xp, mod_p[1], mod_p[0], g_attn_pre, w_in16, tabs_p, True)
    sel = _moba_select(qm, km)
    od = _flash(qdT, kdh, vdT, (lam_vecs, g_col), True, lam_init)
    om = _flash(qmT, kmh, vmT, (sel,), False, lam_init)
    x1_p, h2_p, sT_p = _post_attn(od, om, xp, mod_p[2], mod_p[4], mod_p[3], g_attn_post, g_ffn_pre, w_out16, w_router.T)

    tabs_s = _rope_tables(jnp.full((nseq,), past, dtype=I32))
    qd_s, kd_s, vd_s, qm_s, km_s, vm_s = _qkv(xs.reshape(1, nseq, d), mod_s[1], mod_s[0], g_attn_pre, w_in16, tabs_s, False)
    n_pool, page, ndh, ddh = ckd.shape
    nmh = ckm.shape[2]
    rows_view = lambda c: c.reshape(n_pool, page * ndh, ddh)
    cols_view = lambda c: jnp.transpose(c, (0, 2, 3, 1)).reshape(n_pool, nmh * HEAD_DIM, page)
    per_head = lambda a: a[0].reshape(nseq, ndh, ddh)
    per_mhead = lambda a: a[0].reshape(nseq, nmh, 1, HEAD_DIM)
    od_s = _dec_diff(page_table, per_head(qd_s), per_head(kd_s), per_head(vd_s), rows_view(ckd), rows_view(cvd),
                     lam_vecs, g_subln, lam_init)
    ckmT = cols_view(ckm)
    sel_s = _dec_gate(page_table, qm_s[0], ckmT)
    om_s = _dec_moba(page_table, sel_s[:, :, :MOBA_TOPK].reshape(nseq, -1), per_mhead(qm_s), per_mhead(km_s),
                     per_mhead(vm_s), ckmT, cols_view(cvm))
    x1_s, h2_s, sT_s = _post_attn(od_s.reshape(1, nseq, w), om_s.reshape(1, nseq, w), xs.reshape(1, nseq, d),
                                  mod_s[2], mod_s[4], mod_s[3], g_attn_post, g_ffn_pre, w_out16, w_router.T)

    n_all = b * s + nseq
    h2 = jnp.concatenate([h2_p.reshape(b * s, d), h2_s.reshape(nseq, d)], axis=0)
    idx, pos, wts, cnt = _route(jnp.concatenate([sT_p, sT_s], axis=1), b_router)
    ne = w_router.shape[1]
    counts = cnt[:, 0].astype(I32)
    padded = (counts + MOE_BLOCK - 1) // MOE_BLOCK * MOE_BLOCK
    pad_end = jnp.cumsum(padded)
    pad_start = pad_end - padded
    n_blk = n_all * TOP_K // MOE_BLOCK + ne
    e_iota = jnp.arange(ne, dtype=I32)
    blk_start = jnp.arange(n_blk, dtype=I32) * MOE_BLOCK
    blk_e = jnp.minimum(jnp.sum((pad_end[None, :] <= blk_start[:, None]).astype(I32), axis=1), ne - 1)
    blk_end = jnp.sum(jnp.where(blk_e[:, None] == e_iota[None, :], (pad_start + counts)[None, :], 0), axis=1)
    blk_valid = jnp.clip(blk_end - blk_start, 0, MOE_BLOCK).astype(I32)
    dest = pos + jnp.sum(jnp.where(idx[None] == e_iota[:, None, None], pad_start[:, None, None], 0), axis=0)
    xs_sorted = _dispatch(dest, h2, n_blk * MOE_BLOCK)
    wgu16 = jnp.concatenate([w_exp_gate, w_exp_up], axis=-1).astype(BF16)
    ys = _experts(blk_e, blk_valid, xs_sorted, wgu16, w_exp_down.astype(BF16))
    wsgu16 = jnp.concatenate([w_sh_gate, w_sh_up], axis=-1).astype(BF16)
    wsd16 = w_sh_down.astype(BF16)
    wts_t = wts.T
    y_p = _combine(dest, ys, wts_t, h2, x1_p, mod_p[5], g_ffn_post, wsgu16, wsd16, 0)
    y_s = _combine(dest, ys, wts_t, h2, x1_s, mod_s[5], g_ffn_post, wsgu16, wsd16, b * s)

    hd = ckd.shape[2:]
    hm = ckm.shape[2:]
    rows_p = (kd.reshape((b, s) + hd), vd.reshape((b, s) + hd), km.reshape((b, s) + hm), vm.reshape((b, s) + hm))
    rows_s = (kd_s.reshape((nseq, 1) + hd), vd_s.reshape((nseq, 1) + hd), km_s.reshape((nseq, 1) + hm), vm_s.reshape((nseq, 1) + hm))
    return y_p, y_s.reshape(nseq, 1, d), rows_p, rows_s


def kernel(x_prompt, x_sample, c_prompt, c_sample, cache_k_diff, cache_v_diff, cache_k_moba, cache_v_moba, page_table, w_ada, b_ada, g_attn_pre, g_attn_post, g_ffn_pre, g_ffn_post, w_in, w_out, lambda_q1, lambda_k1, lambda_q2, lambda_k2, g_subln, w_router, b_router, w_exp_gate, w_exp_up, w_exp_down, w_sh_gate, w_sh_up, w_sh_down):
    import math
    depth = w_ada.shape[0]
    yp, ys = x_prompt, x_sample
    rows_p, rows_s = [], []
    for l in range(depth):
        lam_init = 0.8 - 0.6 * math.exp(-0.3 * l)
        yp, ys, rp, rs = _layer(
            yp, ys, c_prompt, c_sample, cache_k_diff[l], cache_v_diff[l], cache_k_moba[l], cache_v_moba[l], page_table,
            lam_init, w_ada[l], b_ada[l], g_attn_pre[l], g_attn_post[l], g_ffn_pre[l], g_ffn_post[l], w_in[l], w_out[l],
            lambda_q1[l], lambda_k1[l], lambda_q2[l], lambda_k2[l], g_subln[l], w_router[l], b_router[l],
            w_exp_gate[l], w_exp_up[l], w_exp_down[l], w_sh_gate[l], w_sh_up[l], w_sh_down[l])
        rows_p.append(rp)
        rows_s.append(rs)
    stack = lambda rows, i: jnp.stack([r[i] for r in rows])
    return (yp, ys, stack(rows_p, 0), stack(rows_p, 1), stack(rows_p, 2), stack(rows_p, 3),
            stack(rows_s, 0), stack(rows_s, 1), stack(rows_s, 2), stack(rows_s, 3))
```

```python
import functools

import jax
import jax.numpy as jnp
from jax import lax
from jax.experimental import pallas as pl
from jax.experimental.pallas import tpu as pltpu

F32 = jnp.float32
BF16 = jnp.bfloat16
I32 = jnp.int32

HEAD_DIM = 64
ROT_DIMS = 16
ROPE_THETA = 500000.0
MOBA_BLOCK = 256
MOBA_TOPK = 3
N_GROUPS = 8
TOPK_GROUPS = 4
TOP_K = 8
ROUTED_SCALE = 2.5
MOE_BLOCK = 256
EPS = 1e-6
NEG = -1e30
LOG2E = 1.4426950408889634

LANES = 128
VMEM_LIMIT = 48 * 1024 * 1024


def _cparams(sem, vmem=VMEM_LIMIT):
    return pltpu.CompilerParams(dimension_semantics=sem, vmem_limit_bytes=vmem)


def _pick(n, candidates):
    for c in candidates:
        if c <= n and n % c == 0:
            return c
    return n


def _split_bf16(a):
    hi = a.astype(BF16)
    lo = (a - hi.astype(F32)).astype(BF16)
    return hi, lo


def _dot3(a, b, dims):
    ah, al = _split_bf16(a)
    bh, bl = _split_bf16(b)
    d = lambda x, y: lax.dot_general(x, y, (dims, ((), ())), preferred_element_type=F32)
    return d(ah, bh) + d(ah, bl) + d(al, bh)


_NN = ((1,), (0,))
_NT = ((1,), (1,))


def _rms(x, g):
    return x * lax.rsqrt(jnp.mean(x * x, axis=-1, keepdims=True) + EPS) * g


def _sigmoid(x):
    return 1.0 / (1.0 + jnp.exp(-x))


def _mod_kernel(c_ref, w_ref, b_ref, o_ref):
    c = c_ref[...]
    s = c * _sigmoid(c)
    o_ref[...] = _dot3(s, w_ref[...], _NN) + b_ref[...]


def _adaln(c, w_ada, b_ada):
    n, d = c.shape
    e = w_ada.shape[1]
    tn = _pick(e, (1024, 512, 256, 128))
    return pl.pallas_call(
        _mod_kernel,
        grid=(e // tn,),
        in_specs=[pl.BlockSpec((n, d), lambda j: (0, 0)),
                  pl.BlockSpec((d, tn), lambda j: (0, j)),
                  pl.BlockSpec((1, tn), lambda j: (0, j))],
        out_specs=pl.BlockSpec((n, tn), lambda j: (0, j)),
        out_shape=jax.ShapeDtypeStruct((n, e), F32),
        compiler_params=_cparams(("arbitrary",)),
    )(c, w_ada, b_ada.reshape(1, e))


def _rope_tables(pos):
    half = ROT_DIMS // 2
    inv_freq = ROPE_THETA ** (-2.0 * jnp.arange(half, dtype=F32) / ROT_DIMS)
    ang = pos.astype(F32)[:, None] * inv_freq[None, :]
    cos, sin = jnp.cos(ang), jnp.sin(ang)
    lane = jnp.arange(LANES) % HEAD_DIM
    idx = lane % half
    c = jnp.where(lane < ROT_DIMS, cos[:, idx], 1.0)
    sa = jnp.where(lane < half, -sin[:, idx], 0.0)
    sb = jnp.where((lane >= half) & (lane < ROT_DIMS), sin[:, idx], 0.0)
    return c.astype(F32), sa.astype(F32), sb.astype(F32)


def _rope(a, c, sa, sb):
    half = ROT_DIMS // 2
    out = []
    for j in range(a.shape[1] // LANES):
        x = a[:, j * LANES:(j + 1) * LANES]
        out.append(x * c + pltpu.roll(x, LANES - half, 1) * sa + pltpu.roll(x, half, 1) * sb)
    return jnp.concatenate(out, axis=1)


def _qkv_kernel(x_ref, sc_ref, sh_ref, g_ref, w_ref, c_ref, sa_ref, sb_ref, *outs, width, attn_layouts):
    x = x_ref[0]
    h = _rms(x, g_ref[...]) * (1.0 + sc_ref[0]) + sh_ref[0]
    proj = jnp.dot(h.astype(BF16), w_ref[...], preferred_element_type=F32)
    c, sa, sb = c_ref[...], sa_ref[...], sb_ref[...]
    w = width
    scale = HEAD_DIM ** -0.5 * (LOG2E if attn_layouts else 1.0)
    qd = _rope(proj[:, 0 * w:1 * w], c, sa, sb) * scale
    kd = _rope(proj[:, 1 * w:2 * w], c, sa, sb)
    vd = proj[:, 2 * w:3 * w]
    qm = _rope(proj[:, 3 * w:4 * w], c, sa, sb) * scale
    km = _rope(proj[:, 4 * w:5 * w], c, sa, sb)
    vm = proj[:, 5 * w:6 * w]
    if not attn_layouts:
        for r, v in zip(outs, (qd, kd, vd, qm, km, vm)):
            r[0] = v
        return
    kd_o, vd_o, km_o, vm_o, qm_o, qdT_o, qmT_o, kdh_o, kmh_o, vdT_o, vmT_o = outs
    ndh = w // (2 * HEAD_DIM)
    for h in range(ndh):
        kd_o[0, pl.ds(h, x.shape[0], stride=ndh), :] = kd[:, h * 2 * HEAD_DIM:(h + 1) * 2 * HEAD_DIM]
        vd_o[0, pl.ds(h, x.shape[0], stride=ndh), :] = vd[:, h * 2 * HEAD_DIM:(h + 1) * 2 * HEAD_DIM]
    km_o[0] = km
    vm_o[0] = vm
    qm_o[0] = qm
    qdT_o[0] = qd.T.astype(BF16)
    qmT_o[0] = qm.T.astype(BF16)
    for m in range(w // HEAD_DIM):
        kdh_o[0, m] = kd[:, m * HEAD_DIM:(m + 1) * HEAD_DIM].astype(BF16)
        kmh_o[0, m] = km[:, m * HEAD_DIM:(m + 1) * HEAD_DIM].astype(BF16)
    for r in range(x.shape[0] // MOBA_BLOCK):
        vdT_o[0, r] = vd[r * MOBA_BLOCK:(r + 1) * MOBA_BLOCK].T.astype(BF16)
        vmT_o[0, r] = vm[r * MOBA_BLOCK:(r + 1) * MOBA_BLOCK].T.astype(BF16)


def _qkv(x, sc, sh, g, w_in16, tabs, attn_layouts):
    b, s, d = x.shape
    w = w_in16.shape[1] // 6
    ts = _pick(s, (512, 256)) if attn_layouts else s
    sm = sc.shape[1]
    mod_spec = pl.BlockSpec((1, 1 if sm == 1 else ts, d), (lambda i, j: (i, 0, 0)) if sm == 1 else (lambda i, j: (i, j, 0)))
    tab_spec = pl.BlockSpec((ts, LANES), lambda i, j: (j, 0))
    row_spec = pl.BlockSpec((1, ts, w), lambda i, j: (i, j, 0))
    row_shape = jax.ShapeDtypeStruct((b, s, w), F32)
    if attn_layouts:
        nb = s // MOBA_BLOCK
        nh = w // HEAD_DIM
        ndh = w // (2 * HEAD_DIM)
        head_rows_spec = pl.BlockSpec((1, ts * ndh, 2 * HEAD_DIM), lambda i, j: (i, j, 0))
        head_rows_shape = jax.ShapeDtypeStruct((b, s * ndh, 2 * HEAD_DIM), F32)
        out_specs = [head_rows_spec] * 2 + [row_spec] * 3 + [
            pl.BlockSpec((1, w, ts), lambda i, j: (i, 0, j)),
            pl.BlockSpec((1, w, ts), lambda i, j: (i, 0, j)),
            pl.BlockSpec((1, nh, ts, HEAD_DIM), lambda i, j: (i, 0, j, 0)),
            pl.BlockSpec((1, nh, ts, HEAD_DIM), lambda i, j: (i, 0, j, 0)),
            pl.BlockSpec((1, ts // MOBA_BLOCK, w, MOBA_BLOCK), lambda i, j: (i, j, 0, 0)),
            pl.BlockSpec((1, ts // MOBA_BLOCK, w, MOBA_BLOCK), lambda i, j: (i, j, 0, 0)),
        ]
        out_shape = [head_rows_shape] * 2 + [row_shape] * 3 + [
            jax.ShapeDtypeStruct((b, w, s), BF16), jax.ShapeDtypeStruct((b, w, s), BF16),
            jax.ShapeDtypeStruct((b, nh, s, HEAD_DIM), BF16), jax.ShapeDtypeStruct((b, nh, s, HEAD_DIM), BF16),
            jax.ShapeDtypeStruct((b, nb, w, MOBA_BLOCK), BF16), jax.ShapeDtypeStruct((b, nb, w, MOBA_BLOCK), BF16),
        ]
    else:
        out_specs = [row_spec] * 6
        out_shape = [row_shape] * 6
    return pl.pallas_call(
        functools.partial(_qkv_kernel, width=w, attn_layouts=attn_layouts),
        grid=(b, s // ts),
        in_specs=[pl.BlockSpec((1, ts, d), lambda i, j: (i, j, 0)), mod_spec, mod_spec,
                  pl.BlockSpec((1, d), lambda i, j: (0, 0)),
                  pl.BlockSpec(w_in16.shape, lambda i, j: (0, 0)),
                  tab_spec, tab_spec, tab_spec],
        out_specs=out_specs,
        out_shape=out_shape,
        compiler_params=_cparams(("arbitrary", "arbitrary")),
    )(x, sc, sh, g.reshape(1, d), w_in16, *tabs)


def _select_kernel(q_ref, k_ref, o_ref, kmean_ref, gate_ref, *, nb, nh):
    j = pl.program_id(1)

    @pl.when(j == 0)
    def _():
        kmean_ref[...] = jnp.zeros_like(kmean_ref)

    kmean_ref[pl.ds(j, 1), :] = jnp.mean(k_ref[0], axis=0, keepdims=True)
    kmean = kmean_ref[...]
    lane_head = lax.broadcasted_iota(I32, kmean.shape, 1) // HEAD_DIM
    kbd = jnp.concatenate([jnp.where(lane_head == h, kmean, 0.0) for h in range(nh)], axis=0)
    gate = _dot3(kbd, q_ref[0], _NT).reshape(nh, nb, -1)
    gate_ref[...] = gate
    row = lax.broadcasted_iota(I32, gate.shape, 1)

    def count(jp, rank):
        other = gate_ref[:, pl.ds(jp, 1), :]
        beats = (other > gate) | ((other == gate) & (jp < row))
        return rank + jnp.where(beats, 1.0, 0.0)

    rank = lax.fori_loop(0, j, count, jnp.zeros(gate.shape, F32))
    o_ref[0] = jnp.where((row < j) & (rank < MOBA_TOPK), 1.0, 0.0)


def _moba_select(qm, km):
    b, s, w = qm.shape
    nb = s // MOBA_BLOCK
    nh = w // HEAD_DIM
    return pl.pallas_call(
        functools.partial(_select_kernel, nb=nb, nh=nh),
        grid=(b, nb),
        in_specs=[pl.BlockSpec((1, MOBA_BLOCK, w), lambda i, j: (i, j, 0)),
                  pl.BlockSpec((1, MOBA_BLOCK, w), lambda i, j: (i, j, 0))],
        out_specs=pl.BlockSpec((1, nh, nb, MOBA_BLOCK), lambda i, j: (i, 0, 0, j)),
        out_shape=jax.ShapeDtypeStruct((b, nh, nb, s), F32),
        scratch_shapes=[pltpu.VMEM((nb, w), F32), pltpu.VMEM((nh, nb, MOBA_BLOCK), F32)],
        compiler_params=_cparams(("arbitrary", "arbitrary")),
    )(qm, km)


def _flash_kernel(*refs, diff, lam_init):
    if diff:
        q_ref, k_ref, v_ref, lam_ref, g_ref, o_ref = refs
        sel_ref = None
    else:
        q_ref, k_ref, v_ref, sel_ref, o_ref = refs
    i = pl.program_id(2)
    tq = q_ref.shape[2]
    tk = MOBA_BLOCK
    nqb = tq // tk
    dv = v_ref.shape[2] if diff else HEAD_DIM
    qs = (q_ref[0, 0:HEAD_DIM, :], q_ref[0, HEAD_DIM:2 * HEAD_DIM, :])

    def step(jj, carry, diagonal):
        new = []
        for s in range(2):
            m, l, acc = carry[s]
            sts = []
            for h in range(nqb):
                blk = jj * nqb + h
                k = k_ref[0, s, pl.ds(pl.multiple_of(blk * tk, tk), tk), :]
                st = jnp.dot(k, qs[s], preferred_element_type=F32)
                if diagonal:
                    key = lax.broadcasted_iota(I32, st.shape, 0) + h * tk
                    qry = lax.broadcasted_iota(I32, st.shape, 1)
                    keep = key <= qry
                    if sel_ref is not None and h < nqb - 1:
                        keep = keep & ((qry < (h + 1) * tk) | (sel_ref[0, s, pl.ds(blk, 1), :] > 0.0))
                    st = jnp.where(keep, st, NEG)
                elif sel_ref is not None:
                    st = jnp.where(sel_ref[0, s, pl.ds(blk, 1), :] > 0.0, st, NEG)
                sts.append(st)
            m_new = m
            for st in sts:
                m_new = jnp.maximum(m_new, jnp.max(st, axis=0, keepdims=True))
            alpha = jnp.exp2(m - m_new)
            l = alpha * l
            acc = alpha * acc
            for h, st in enumerate(sts):
                blk = jj * nqb + h
                p = jnp.exp2(st - m_new)
                l = l + jnp.sum(p, axis=0, keepdims=True)
                v = v_ref[0, blk] if diff else v_ref[0, blk, s * HEAD_DIM:(s + 1) * HEAD_DIM, :]
                acc = acc + jnp.dot(v, p.astype(BF16), preferred_element_type=F32)
            new.append((m_new, l, acc))
        return tuple(new)

    init = tuple((jnp.full((1, tq), NEG, F32), jnp.zeros((1, tq), F32), jnp.zeros((dv, tq), F32)) for _ in range(2))
    carry = lax.fori_loop(0, i, lambda jj, c: step(jj, c, False), init)
    (m0, l0, a0), (m1, l1, a1) = step(i, carry, True)
    o0 = a0 / l0
    o1 = a1 / l1
    if diff:
        lv = lam_ref[...]
        lam = (jnp.exp(jnp.sum(lv[0:1] * lv[1:2], axis=1, keepdims=True))
               - jnp.exp(jnp.sum(lv[2:3] * lv[3:4], axis=1, keepdims=True)) + lam_init)
        o = o0 - lam * o1
        o = o * lax.rsqrt(jnp.mean(o * o, axis=0, keepdims=True) + EPS) * g_ref[...] * (1.0 - lam_init)
    else:
        o = jnp.concatenate([o0, o1], axis=0)
    o_ref[0] = o.T.astype(o_ref.dtype)


FLASH_TQ = 1024


def _flash(qT, k_hm, vT, extra, diff, lam_init):
    b, w, s = qT.shape
    nb = s // MOBA_BLOCK
    tq = FLASH_TQ if s % FLASH_TQ == 0 else MOBA_BLOCK
    ng = w // (2 * HEAD_DIM)
    in_specs = [pl.BlockSpec((1, 2 * HEAD_DIM, tq), lambda bi, g, i: (bi, g, i)),
                pl.BlockSpec((1, 2, s, HEAD_DIM), lambda bi, g, i: (bi, g, 0, 0)),
                pl.BlockSpec((1, nb, 2 * HEAD_DIM, MOBA_BLOCK), lambda bi, g, i: (bi, 0, g, 0))]
    if diff:
        lam_vecs, g_sub = extra
        in_specs += [pl.BlockSpec(lam_vecs.shape, lambda bi, g, i: (0, 0)),
                     pl.BlockSpec(g_sub.shape, lambda bi, g, i: (0, 0))]
        args = (lam_vecs, g_sub)
    else:
        (sel,) = extra
        in_specs += [pl.BlockSpec((1, 2, nb, tq), lambda bi, g, i: (bi, g, 0, i))]
        args = (sel,)
    return pl.pallas_call(
        functools.partial(_flash_kernel, diff=diff, lam_init=lam_init),
        grid=(b, ng, s // tq),
        in_specs=in_specs,
        out_specs=pl.BlockSpec((1, tq, 2 * HEAD_DIM), lambda bi, g, i: (bi, i, g)),
        out_shape=jax.ShapeDtypeStruct((b, s, w), BF16),
        compiler_params=_cparams(("arbitrary", "arbitrary", "arbitrary")),
    )(qT, k_hm, vT, *args)


def _post_kernel(od_ref, om_ref, x_ref, gt_ref, sc_ref, sh_ref, ga_ref, gf_ref, w_ref, wr_ref,
                 x1_ref, h2_ref, sc_out_ref):
    w = od_ref.shape[2]
    y = (jnp.dot(od_ref[0].astype(BF16), w_ref[0:w, :], preferred_element_type=F32)
         + jnp.dot(om_ref[0].astype(BF16), w_ref[w:2 * w, :], preferred_element_type=F32))
    x1 = x_ref[0] + gt_ref[0] * _rms(y, ga_ref[...])
    h2 = _rms(x1, gf_ref[...]) * (1.0 + sc_ref[0]) + sh_ref[0]
    x1_ref[0] = x1
    h2_ref[0] = h2
    sc_out_ref[...] = _sigmoid(_dot3(wr_ref[...], h2, _NT))


def _post_attn(od, om, x, gt, sc, sh, g_post, g_pre, w_out16, w_rT):
    b, s, d = x.shape
    w = od.shape[2]
    ne = w_rT.shape[0]
    ts = _pick(s, (512, 256, 128))
    sm = gt.shape[1]
    mod_spec = pl.BlockSpec((1, 1 if sm == 1 else ts, d), (lambda i, j: (i, 0, 0)) if sm == 1 else (lambda i, j: (i, j, 0)))
    vec_spec = pl.BlockSpec((1, d), lambda i, j: (0, 0))
    tok_spec = pl.BlockSpec((1, ts, d), lambda i, j: (i, j, 0))
    ns = s // ts
    return pl.pallas_call(
        _post_kernel,
        grid=(b, ns),
        in_specs=[pl.BlockSpec((1, ts, w), lambda i, j: (i, j, 0)), pl.BlockSpec((1, ts, w), lambda i, j: (i, j, 0)),
                  tok_spec, mod_spec, mod_spec, mod_spec, vec_spec, vec_spec,
                  pl.BlockSpec(w_out16.shape, lambda i, j: (0, 0)),
                  pl.BlockSpec(w_rT.shape, lambda i, j: (0, 0))],
        out_specs=[tok_spec, tok_spec, pl.BlockSpec((ne, ts), lambda i, j: (0, i * ns + j))],
        out_shape=[jax.ShapeDtypeStruct((b, s, d), F32), jax.ShapeDtypeStruct((b, s, d), F32),
                   jax.ShapeDtypeStruct((ne, b * s), F32)],
        compiler_params=_cparams(("arbitrary", "arbitrary")),
    )(od, om, x, gt, sc, sh, g_post.reshape(1, d), g_pre.reshape(1, d), w_out16, w_rT)


def _take_max(x, iota, fill):
    m = jnp.max(x, axis=0, keepdims=True)
    idx = jnp.min(jnp.where(x == m, iota, fill), axis=0, keepdims=True)
    hot = iota == idx
    return m, idx, hot, jnp.where(hot, -jnp.inf, x)


def _route_kernel(s_ref, b_ref, tri_ref, idx_ref, pos_ref, w_ref, cnt_ref, run_ref):
    step = pl.program_id(0)

    @pl.when(step == 0)
    def _():
        run_ref[...] = jnp.zeros_like(run_ref)

    scores = s_ref[...]
    ne, tn = scores.shape
    gs = ne // N_GROUPS
    biased = scores + b_ref[...]
    grp = biased.reshape(N_GROUPS, gs, tn)
    sub = lax.broadcasted_iota(I32, grp.shape, 1)
    m1 = jnp.max(grp, axis=1, keepdims=True)
    i1 = jnp.min(jnp.where(grp == m1, sub, gs), axis=1, keepdims=True)
    m2 = jnp.max(jnp.where(sub == i1, -jnp.inf, grp), axis=1, keepdims=True)
    gscore = (m1 + m2).reshape(N_GROUPS, tn)
    giota = lax.broadcasted_iota(I32, gscore.shape, 0)
    gmask = jnp.zeros(gscore.shape, jnp.bool_)
    for _ in range(TOPK_GROUPS):
        _, _, hot, gscore = _take_max(gscore, giota, N_GROUPS)
        gmask = gmask | hot
    emask = jnp.broadcast_to(gmask.reshape(N_GROUPS, 1, tn), (N_GROUPS, gs, tn)).reshape(ne, tn)
    cand = jnp.where(emask, biased, -jnp.inf)
    eiota = lax.broadcasted_iota(I32, cand.shape, 0)
    chosen = jnp.zeros(cand.shape, F32)
    idxs, wts = [], []
    for _ in range(TOP_K):
        _, idx, hot, cand = _take_max(cand, eiota, ne)
        idxs.append(idx)
        wts.append(jnp.sum(jnp.where(hot, scores, 0.0), axis=0, keepdims=True))
        chosen = chosen + jnp.where(hot, 1.0, 0.0)
    wsum = wts[0]
    for t in wts[1:]:
        wsum = wsum + t
    before = run_ref[...] + jnp.dot(chosen.astype(BF16), tri_ref[...], preferred_element_type=F32)
    poss = [jnp.sum(jnp.where(eiota == idx, before, 0.0), axis=0, keepdims=True) for idx in idxs]
    idx_ref[...] = jnp.concatenate(idxs, axis=0)
    pos_ref[...] = jnp.concatenate(poss, axis=0).astype(I32)
    w_ref[...] = jnp.concatenate(wts, axis=0) / wsum * ROUTED_SCALE
    run = run_ref[...] + jnp.sum(chosen, axis=1, keepdims=True)
    run_ref[...] = run
    cnt_ref[...] = jnp.broadcast_to(run, cnt_ref.shape)


def _route(scoresT, b_router):
    ne, n = scoresT.shape
    tn = _pick(n, (1152, 1024, 512, 384, 256, 128))
    tri = jnp.triu(jnp.ones((tn, tn), BF16), k=1)
    blk = pl.BlockSpec((TOP_K, tn), lambda i: (0, i))
    return pl.pallas_call(
        _route_kernel,
        grid=(n // tn,),
        in_specs=[pl.BlockSpec((ne, tn), lambda i: (0, i)),
                  pl.BlockSpec((ne, 1), lambda i: (0, 0)),
                  pl.BlockSpec((tn, tn), lambda i: (0, 0))],
        out_specs=[blk, blk, blk, pl.BlockSpec((ne, LANES), lambda i: (0, 0))],
        out_shape=[jax.ShapeDtypeStruct((TOP_K, n), I32), jax.ShapeDtypeStruct((TOP_K, n), I32),
                   jax.ShapeDtypeStruct((TOP_K, n), F32), jax.ShapeDtypeStruct((ne, LANES), F32)],
        scratch_shapes=[pltpu.VMEM((ne, 1), F32)],
        compiler_params=_cparams(("arbitrary",)),
    )(scoresT, b_router.reshape(ne, 1), tri)


def _row_copy(src, si, dst, di, sem):
    return pltpu.make_async_copy(src.at[pl.ds(si, 1)], dst.at[pl.ds(di, 1)], sem)


def _dispatch_kernel(dest_ref, h_ref, xs_ref, sem):
    tn = h_ref.shape[0]

    def issue(t, _):
        for k in range(TOP_K):
            _row_copy(h_ref, t, xs_ref, dest_ref[k, t], sem).start()
        return _

    lax.fori_loop(0, tn, issue, 0)

    def drain(t, _):
        for k in range(TOP_K):
            _row_copy(h_ref, t, xs_ref, dest_ref[k, t], sem).wait()
        return _

    lax.fori_loop(0, tn, drain, 0)


def _dispatch(dest, h2, n_rows):
    n, d = h2.shape
    tn = _pick(n, (128,))
    return pl.pallas_call(
        _dispatch_kernel,
        grid=(n // tn,),
        in_specs=[pl.BlockSpec((TOP_K, tn), lambda i: (0, i), memory_space=pltpu.SMEM),
                  pl.BlockSpec((tn, d), lambda i: (i, 0))],
        out_specs=pl.BlockSpec(memory_space=pl.ANY),
        out_shape=jax.ShapeDtypeStruct((n_rows, d), F32),
        scratch_shapes=[pltpu.SemaphoreType.DMA],
        compiler_params=_cparams(("arbitrary",)),
    )(dest, h2)


def _expert_kernel(be_ref, nv_ref, x_ref, wgu_ref, wd_ref, o_ref):
    del be_ref
    de = wd_ref.shape[1]
    x = x_ref[...]
    x = jnp.where(lax.broadcasted_iota(I32, x.shape, 0) < nv_ref[pl.program_id(0)], x, 0.0)
    gu = jnp.dot(x.astype(BF16), wgu_ref[0], preferred_element_type=F32)
    g, u = gu[:, :de], gu[:, de:]
    mid = g * _sigmoid(g) * u
    o_ref[...] = jnp.dot(mid.astype(BF16), wd_ref[0], preferred_element_type=F32)


def _experts(blk_e, blk_valid, xs, wgu16, wd16):
    n_rows, d = xs.shape
    de = wd16.shape[1]
    return pl.pallas_call(
        _expert_kernel,
        grid_spec=pltpu.PrefetchScalarGridSpec(
            num_scalar_prefetch=2,
            grid=(n_rows // MOE_BLOCK,),
            in_specs=[pl.BlockSpec((MOE_BLOCK, d), lambda i, be, nv: (i, 0)),
                      pl.BlockSpec((1, d, 2 * de), lambda i, be, nv: (be[i], 0, 0)),
                      pl.BlockSpec((1, de, d), lambda i, be, nv: (be[i], 0, 0))],
            out_specs=pl.BlockSpec((MOE_BLOCK, d), lambda i, be, nv: (i, 0))),
        out_shape=jax.ShapeDtypeStruct((n_rows, d), F32),
        compiler_params=_cparams(("arbitrary",)),
    )(blk_e, blk_valid, xs, wgu16, wd16)


def _combine_kernel(dest_ref, ys_ref, w_ref, h_ref, x1_ref, gt_ref, g_ref, wsgu_ref, wsd_ref, o_ref, rows_ref, sem):
    tn = h_ref.shape[0]

    def issue(t, _):
        for k in range(TOP_K):
            _row_copy(ys_ref, dest_ref[k, t], rows_ref.at[k], t, sem).start()
        return _

    lax.fori_loop(0, tn, issue, 0)
    de = wsd_ref.shape[0]
    gu = jnp.dot(h_ref[...].astype(BF16), wsgu_ref[...], preferred_element_type=F32)
    g, u = gu[:, :de], gu[:, de:]
    f = jnp.dot((g * _sigmoid(g) * u).astype(BF16), wsd_ref[...], preferred_element_type=F32)

    def drain(t, _):
        for k in range(TOP_K):
            _row_copy(ys_ref, dest_ref[k, t], rows_ref.at[k], t, sem).wait()
        return _

    lax.fori_loop(0, tn, drain, 0)
    w = w_ref[...]
    for k in range(TOP_K):
        f = f + rows_ref[k] * w[:, k:k + 1]
    o_ref[0] = x1_ref[0] + gt_ref[0] * _rms(f, g_ref[...])


def _combine(dest, ys, wts, h2, x1, gt, g_post, wsgu16, wsd16, tok0):
    b, s, d = x1.shape
    tn = _pick(s, (128,))
    ns = s // tn
    off = tok0 // tn
    sm = gt.shape[1]
    mod_spec = pl.BlockSpec((1, 1 if sm == 1 else tn, d), (lambda i, j: (i, 0, 0)) if sm == 1 else (lambda i, j: (i, j, 0)))
    return pl.pallas_call(
        _combine_kernel,
        grid=(b, ns),
        in_specs=[pl.BlockSpec((TOP_K, tn), lambda i, j: (0, off + i * ns + j), memory_space=pltpu.SMEM),
                  pl.BlockSpec(memory_space=pl.ANY),
                  pl.BlockSpec((tn, TOP_K), lambda i, j: (off + i * ns + j, 0)),
                  pl.BlockSpec((tn, d), lambda i, j: (off + i * ns + j, 0)),
                  pl.BlockSpec((1, tn, d), lambda i, j: (i, j, 0)),
                  mod_spec,
                  pl.BlockSpec((1, d), lambda i, j: (0, 0)),
                  pl.BlockSpec(wsgu16.shape, lambda i, j: (0, 0)),
                  pl.BlockSpec(wsd16.shape, lambda i, j: (0, 0))],
        out_specs=pl.BlockSpec((1, tn, d), lambda i, j: (i, j, 0)),
        out_shape=jax.ShapeDtypeStruct((b, s, d), F32),
        scratch_shapes=[pltpu.VMEM((TOP_K, tn, d), F32), pltpu.SemaphoreType.DMA],
        compiler_params=_cparams(("arbitrary", "arbitrary")),
    )(dest, ys, wts, h2, x1, gt, g_post.reshape(1, d), wsgu16, wsd16)


PAGES_PER_STEP = 8
GATE_PAGES_PER_STEP = 16


def _per_map(x, nmaps):
    row_head = lax.broadcasted_iota(I32, (nmaps, x.shape[1]), 0) // 2
    out = jnp.zeros((nmaps, x.shape[1]), x.dtype)
    for h in range(x.shape[0]):
        out = jnp.where(row_head == h, x[h:h + 1, :], out)
    return out


def _dec_diff_kernel(pt_ref, q_ref, kn_ref, vn_ref, lam_ref, g_ref, *refs, lam_init, nheads):
    del pt_ref
    k_refs = refs[:PAGES_PER_STEP]
    v_refs = refs[PAGES_PER_STEP:2 * PAGES_PER_STEP]
    o_ref, m_ref, l_ref, acc_ref = refs[2 * PAGES_PER_STEP:]
    p = pl.program_id(1)
    nmaps = 2 * nheads
    dv = q_ref.shape[2]
    lane_map = lax.broadcasted_iota(I32, (nmaps, dv), 1) // HEAD_DIM
    row_map = lax.broadcasted_iota(I32, (nmaps, dv), 0) % 2
    q8 = jnp.where(lane_map == row_map, _per_map(q_ref[0], nmaps), 0.0)
    q16 = q8.astype(BF16)

    @pl.when(p == 0)
    def _():
        m_ref[...] = jnp.sum(q8 * _per_map(kn_ref[0], nmaps), axis=1, keepdims=True)
        l_ref[...] = jnp.ones_like(l_ref)
        acc_ref[...] = _per_map(vn_ref[0], nmaps)

    rows = k_refs[0].shape[1]
    own = (lax.broadcasted_iota(I32, (nmaps, rows), 1) % nheads) == (lax.broadcasted_iota(I32, (nmaps, rows), 0) // 2)
    st = jnp.concatenate(
        [jnp.where(own, lax.dot_general(q16, k_refs[r][0].astype(BF16), (_NT, ((), ())), preferred_element_type=F32), NEG)
         for r in range(PAGES_PER_STEP)], axis=1)
    m_old = m_ref[...]
    m_new = jnp.maximum(m_old, jnp.max(st, axis=1, keepdims=True))
    alpha = jnp.exp(m_old - m_new)
    pr = jnp.exp(st - m_new)
    l_ref[...] = alpha * l_ref[...] + jnp.sum(pr, axis=1, keepdims=True)
    pr16 = pr.astype(BF16)
    pv = jnp.zeros(acc_ref.shape, F32)
    for r in range(PAGES_PER_STEP):
        pv = pv + jnp.dot(pr16[:, r * rows:(r + 1) * rows], v_refs[r][0].astype(BF16), preferred_element_type=F32)
    acc_ref[...] = alpha * acc_ref[...] + pv
    m_ref[...] = m_new

    @pl.when(p == pl.num_programs(1) - 1)
    def _():
        o = acc_ref[...] / l_ref[...]
        lv = lam_ref[...]
        lam = (jnp.exp(jnp.sum(lv[0:1] * lv[1:2], axis=1, keepdims=True))
               - jnp.exp(jnp.sum(lv[2:3] * lv[3:4], axis=1, keepdims=True)) + lam_init)
        outs = [_rms(o[2 * h:2 * h + 1] - lam * o[2 * h + 1:2 * h + 2], g_ref[...]) * (1.0 - lam_init) for h in range(nheads)]
        o_ref[0] = jnp.concatenate(outs, axis=0)


def _dec_diff(page_table, q, k_new, v_new, cache_k, cache_v, lam_vecs, g_sub, lam_init):
    nseq, nheads, dv = q.shape
    npages = page_table.shape[1]
    rows = cache_k.shape[1]
    nsteps = npages // PAGES_PER_STEP

    def page_spec(r):
        return pl.BlockSpec((1, rows, dv), lambda b, p, pt: (pt[b, p * PAGES_PER_STEP + r], 0, 0))

    seq_spec = pl.BlockSpec((1, nheads, dv), lambda b, p, pt: (b, 0, 0))
    return pl.pallas_call(
        functools.partial(_dec_diff_kernel, lam_init=lam_init, nheads=nheads),
        grid_spec=pltpu.PrefetchScalarGridSpec(
            num_scalar_prefetch=1,
            grid=(nseq, nsteps),
            in_specs=[seq_spec, seq_spec, seq_spec,
                      pl.BlockSpec(lam_vecs.shape, lambda b, p, pt: (0, 0)),
                      pl.BlockSpec((1, dv), lambda b, p, pt: (0, 0))]
                     + [page_spec(r) for r in range(PAGES_PER_STEP)] * 2,
            out_specs=seq_spec,
            scratch_shapes=[pltpu.VMEM((2 * nheads, 1), F32), pltpu.VMEM((2 * nheads, 1), F32),
                            pltpu.VMEM((2 * nheads, dv), F32)]),
        out_shape=jax.ShapeDtypeStruct((nseq, nheads, dv), F32),
        compiler_params=_cparams(("arbitrary", "arbitrary")),
    )(page_table, q, k_new, v_new, lam_vecs, g_sub.reshape(1, dv),
      *([cache_k] * PAGES_PER_STEP), *([cache_v] * PAGES_PER_STEP))


def _dec_gate_kernel(pt_ref, q_ref, *refs, nh, pages_per_block, nblocks):
    del pt_ref
    pps = len(refs) - 2
    k_refs = refs[:pps]
    o_ref, gate_ref = refs[pps:]
    p = pl.program_id(1)

    @pl.when(p == 0)
    def _():
        gate_ref[...] = jnp.zeros_like(gate_ref)

    qb = q_ref[0]
    lane = lax.broadcasted_iota(I32, (nh, LANES), 1)
    g = gate_ref[...]
    for r in range(0, pps, pages_per_block):
        tot = k_refs[r][0]
        for e in range(1, pages_per_block):
            tot = tot + k_refs[r + e][0]
        prod = tot * qb
        per_head = jnp.concatenate(
            [jnp.sum(prod[h * HEAD_DIM:(h + 1) * HEAD_DIM], axis=0, keepdims=True) for h in range(nh)], axis=0)
        blk = (p * pps + r) // pages_per_block
        g = g + jnp.where(lane == blk, jnp.sum(per_head, axis=1, keepdims=True), 0.0)
    gate_ref[...] = g

    @pl.when(p == pl.num_programs(1) - 1)
    def _():
        gate = jnp.where(lane < nblocks, g, -jnp.inf)
        out = jnp.zeros((nh, LANES), I32)
        for r in range(MOBA_TOPK):
            m = jnp.max(gate, axis=1, keepdims=True)
            idx = jnp.min(jnp.where(gate == m, lane, LANES), axis=1, keepdims=True)
            out = jnp.where(lane == r, idx, out)
            gate = jnp.where(lane == idx, -jnp.inf, gate)
        o_ref[0] = out


def _dec_gate(page_table, q, cache_kT):
    nseq, w = q.shape
    npages = page_table.shape[1]
    page = cache_kT.shape[2]
    nh = w // HEAD_DIM
    ppb = MOBA_BLOCK // page
    nblocks = npages // ppb
    pps = _pick(npages, (GATE_PAGES_PER_STEP, PAGES_PER_STEP))
    assert nblocks <= LANES and pps % ppb == 0

    def page_spec(r):
        return pl.BlockSpec((1, w, page), lambda b, p, pt: (pt[b, p * pps + r], 0, 0))

    return pl.pallas_call(
        functools.partial(_dec_gate_kernel, nh=nh, pages_per_block=ppb, nblocks=nblocks),
        grid_spec=pltpu.PrefetchScalarGridSpec(
            num_scalar_prefetch=1,
            grid=(nseq, npages // pps),
            in_specs=[pl.BlockSpec((1, w, page), lambda b, p, pt: (b, 0, 0))]
                     + [page_spec(r) for r in range(pps)],
            out_specs=pl.BlockSpec((1, nh, LANES), lambda b, p, pt: (b, 0, 0)),
            scratch_shapes=[pltpu.VMEM((nh, LANES), F32)]),
        out_shape=jax.ShapeDtypeStruct((nseq, nh, LANES), I32),
        compiler_params=_cparams(("arbitrary", "arbitrary")),
    )(page_table, jnp.broadcast_to(q[:, :, None], (nseq, w, page)), *([cache_kT] * pps))


def _dec_moba_kernel(pt_ref, sel_ref, q_ref, kn_ref, vn_ref, *refs, npg):
    del pt_ref, sel_ref
    k_refs = refs[:npg]
    v_refs = refs[npg:2 * npg]
    o_ref = refs[-1]
    q = q_ref[0, 0]
    q8 = jnp.where(lax.broadcasted_iota(I32, (8, HEAD_DIM), 0) == 0, jnp.broadcast_to(q, (8, HEAD_DIM)), 0.0)
    q16 = q8.astype(BF16)
    st = jnp.concatenate([jnp.dot(q16, k_refs[r][0].astype(BF16), preferred_element_type=F32) for r in range(npg)],
                         axis=1)
    s_new = jnp.sum(q8 * kn_ref[0, 0], axis=1, keepdims=True)
    m = jnp.maximum(jnp.max(st, axis=1, keepdims=True), s_new)
    pr = jnp.exp(st - m)
    p_new = jnp.exp(s_new - m)
    l = jnp.sum(pr, axis=1, keepdims=True) + p_new
    pr16 = pr.astype(BF16)
    page = k_refs[0].shape[2]
    acc = p_new * vn_ref[0, 0]
    for r in range(npg):
        acc = acc + lax.dot_general(pr16[:, r * page:(r + 1) * page], v_refs[r][0].astype(BF16), (_NT, ((), ())),
                                    preferred_element_type=F32)
    o_ref[0, 0] = (acc / l)[0:1]


def _dec_moba(page_table, sel, q, k_new, v_new, cache_kT, cache_vT):
    nseq, nh = q.shape[:2]
    page = cache_kT.shape[2]
    ppb = MOBA_BLOCK // page
    npg = MOBA_TOPK * ppb

    def page_spec(r):
        def imap(b, h, pt, sl):
            blk = sl[b, h * MOBA_TOPK + r // ppb]
            return (pt[b, blk * ppb + r % ppb], h, 0)
        return pl.BlockSpec((1, HEAD_DIM, page), imap)

    seq_spec = pl.BlockSpec((1, 1, 1, HEAD_DIM), lambda b, h, pt, sl: (b, h, 0, 0))
    return pl.pallas_call(
        functools.partial(_dec_moba_kernel, npg=npg),
        grid_spec=pltpu.PrefetchScalarGridSpec(
            num_scalar_prefetch=2,
            grid=(nseq, nh),
            in_specs=[seq_spec, seq_spec, seq_spec] + [page_spec(r) for r in range(npg)] * 2,
            out_specs=seq_spec),
        out_shape=jax.ShapeDtypeStruct((nseq, nh, 1, HEAD_DIM), F32),
        compiler_params=_cparams(("arbitrary", "arbitrary")),
    )(page_table, sel, q, k_new, v_new, *([cache_kT] * npg), *([cache_vT] * npg))


def _layer(xp, xs, cp, cs, ckd, cvd, ckm, cvm, page_table, lam_init, w_ada, b_ada, g_attn_pre, g_attn_post,
           g_ffn_pre, g_ffn_post, w_in, w_out, lq1, lk1, lq2, lk2, g_subln, w_router, b_router,
           w_exp_gate, w_exp_up, w_exp_down, w_sh_gate, w_sh_up, w_sh_down):
    b, s, d = xp.shape
    nseq = xs.shape[0]
    past = page_table.shape[1] * ckd.shape[1]
    w = w_in.shape[1] // 6
    assert s % MOBA_BLOCK == 0 and MOBA_BLOCK % ckd.shape[1] == 0
    assert page_table.shape[1] % PAGES_PER_STEP == 0 and past // MOBA_BLOCK >= MOBA_TOPK and past % MOBA_BLOCK == 0
    assert xs.shape[1] == 1 and nseq % LANES == 0 and (b * s) % LANES == 0

    mod = _adaln(jnp.concatenate([cp, cs], axis=0), w_ada, b_ada)
    mod_p = [m[:, None, :] for m in jnp.split(mod[:b], 6, axis=-1)]
    mod_s = [m[None, :, :] for m in jnp.split(mod[b:], 6, axis=-1)]

    w_in16 = w_in.astype(BF16)
    w_out16 = w_out.astype(BF16)
    lam_vecs = jnp.stack([lq1, lk1, lq2, lk2]).astype(F32)
    g_col = g_subln.reshape(-1, 1)

    tabs_p = _rope_tables(jnp.arange(s, dtype=I32))
    kd, vd, km, vm, qm, qdT, qmT, kdh, kmh, vdT, vmT = _qkv(xp, mod_p[1], mod_p[0], g_attn_pre, w_in16, tabs_p, True)
    sel = _moba_select(qm, km)
    od = _flash(qdT, kdh, vdT, (lam_vecs, g_col), True, lam_init)
    om = _flash(qmT, kmh, vmT, (sel,), False, lam_init)
    x1_p, h2_p, sT_p = _post_attn(od, om, xp, mod_p[2], mod_p[4], mod_p[3], g_attn_post, g_ffn_pre, w_out16, w_router.T)

    tabs_s = _rope_tables(jnp.full((nseq,), past, dtype=I32))
    qd_s, kd_s, vd_s, qm_s, km_s, vm_s = _qkv(xs.reshape(1, nseq, d), mod_s[1], mod_s[0], g_attn_pre, w_in16, tabs_s, False)
    n_pool, page, ndh, ddh = ckd.shape
    nmh = ckm.shape[2]
    rows_view = lambda c: c.reshape(n_pool, page * ndh, ddh)
    cols_view = lambda c: jnp.transpose(c, (0, 2, 3, 1)).reshape(n_pool, nmh * HEAD_DIM, page)
    per_head = lambda a: a[0].reshape(nseq, ndh, ddh)
    per_mhead = lambda a: a[0].reshape(nseq, nmh, 1, HEAD_DIM)
    od_s = _dec_diff(page_table, per_head(qd_s), per_head(kd_s), per_head(vd_s), rows_view(ckd), rows_view(cvd),
                     lam_vecs, g_subln, lam_init)
    ckmT = cols_view(ckm)
    sel_s = _dec_gate(page_table, qm_s[0], ckmT)
    om_s = _dec_moba(page_table, sel_s[:, :, :MOBA_TOPK].reshape(nseq, -1), per_mhead(qm_s), per_mhead(km_s),
                     per_mhead(vm_s), ckmT, cols_view(cvm))
    x1_s, h2_s, sT_s = _post_attn(od_s.reshape(1, nseq, w), om_s.reshape(1, nseq, w), xs.reshape(1, nseq, d),
                                  mod_s[2], mod_s[4], mod_s[3], g_attn_post, g_ffn_pre, w_out16, w_router.T)

    n_all = b * s + nseq
    h2 = jnp.concatenate([h2_p.reshape(b * s, d), h2_s.reshape(nseq, d)], axis=0)
    idx, pos, wts, cnt = _route(jnp.concatenate([sT_p, sT_s], axis=1), b_router)
    ne = w_router.shape[1]
    counts = cnt[:, 0].astype(I32)
    padded = (counts + MOE_BLOCK - 1) // MOE_BLOCK * MOE_BLOCK
    pad_end = jnp.cumsum(padded)
    pad_start = pad_end - padded
    n_blk = n_all * TOP_K // MOE_BLOCK + ne
    e_iota = jnp.arange(ne, dtype=I32)
    blk_start = jnp.arange(n_blk, dtype=I32) * MOE_BLOCK
    blk_e = jnp.minimum(jnp.sum((pad_end[None, :] <= blk_start[:, None]).astype(I32), axis=1), ne - 1)
    blk_end = jnp.sum(jnp.where(blk_e[:, None] == e_iota[None, :], (pad_start + counts)[None, :], 0), axis=1)
    blk_valid = jnp.clip(blk_end - blk_start, 0, MOE_BLOCK).astype(I32)
    dest = pos + jnp.sum(jnp.where(idx[None] == e_iota[:, None, None], pad_start[:, None, None], 0), axis=0)
    xs_sorted = _dispatch(dest, h2, n_blk * MOE_BLOCK)
    wgu16 = jnp.concatenate([w_exp_gate, w_exp_up], axis=-1).astype(BF16)
    ys = _experts(blk_e, blk_valid, xs_sorted, wgu16, w_exp_down.astype(BF16))
    wsgu16 = jnp.concatenate([w_sh_gate, w_sh_up], axis=-1).astype(BF16)
    wsd16 = w_sh_down.astype(BF16)
    wts_t = wts.T
    y_p = _combine(dest, ys, wts_t, h2, x1_p, mod_p[5], g_ffn_post, wsgu16, wsd16, 0)
    y_s = _combine(dest, ys, wts_t, h2, x1_s, mod_s[5], g_ffn_post, wsgu16, wsd16, b * s)

    hd = ckd.shape[2:]
    hm = ckm.shape[2:]
    rows_p = (kd.reshape((b, s) + hd), vd.reshape((b, s) + hd), km.reshape((b, s) + hm), vm.reshape((b, s) + hm))
    rows_s = (kd_s.reshape((nseq, 1) + hd), vd_s.reshape((nseq, 1) + hd), km_s.reshape((nseq, 1) + hm), vm_s.reshape((nseq, 1) + hm))
    return y_p, y_s.reshape(nseq, 1, d), rows_p, rows_s


def kernel(x_prompt, x_sample, c_prompt, c_sample, cache_k_diff, cache_v_diff, cache_k_moba, cache_v_moba, page_table, w_ada, b_ada, g_attn_pre, g_attn_post, g_ffn_pre, g_ffn_post, w_in, w_out, lambda_q1, lambda_k1, lambda_q2, lambda_k2, g_subln, w_router, b_router, w_exp_gate, w_exp_up, w_exp_down, w_sh_gate, w_sh_up, w_sh_down):
    import math
    depth = w_ada.shape[0]
    yp, ys = x_prompt, x_sample
    rows_p, rows_s = [], []
    for l in range(depth):
        lam_init = 0.8 - 0.6 * math.exp(-0.3 * l)
        yp, ys, rp, rs = _layer(
            yp, ys, c_prompt, c_sample, cache_k_diff[l], cache_v_diff[l], cache_k_moba[l], cache_v_moba[l], page_table,
            lam_init, w_ada[l], b_ada[l], g_attn_pre[l], g_attn_post[l], g_ffn_pre[l], g_ffn_post[l], w_in[l], w_out[l],
            lambda_q1[l], lambda_k1[l], lambda_q2[l], lambda_k2[l], g_subln[l], w_router[l], b_router[l],
            w_exp_gate[l], w_exp_up[l], w_exp_down[l], w_sh_gate[l], w_sh_up[l], w_sh_down[l])
        rows_p.append(rp)
        rows_s.append(rs)
    stack = lambda rows, i: jnp.stack([r[i] for r in rows])
    return (yp, ys, stack(rows_p, 0), stack(rows_p, 1), stack(rows_p, 2), stack(rows_p, 3),
            stack(rows_s, 0), stack(rows_s, 1), stack(rows_s, 2), stack(rows_s, 3))
```

```python
import functools

import jax
import jax.numpy as jnp
from jax import lax
from jax.experimental import pallas as pl
from jax.experimental.pallas import tpu as pltpu

F32 = jnp.float32
BF16 = jnp.bfloat16
I32 = jnp.int32

HEAD_DIM = 64
ROT_DIMS = 16
ROPE_THETA = 500000.0
MOBA_BLOCK = 256
MOBA_TOPK = 3
N_GROUPS = 8
TOPK_GROUPS = 4
TOP_K = 8
ROUTED_SCALE = 2.5
MOE_BLOCK = 256
EPS = 1e-6
NEG = -1e30
LOG2E = 1.4426950408889634

LANES = 128
VMEM_LIMIT = 48 * 1024 * 1024


def _cparams(sem, vmem=VMEM_LIMIT):
    return pltpu.CompilerParams(dimension_semantics=sem, vmem_limit_bytes=vmem)


def _pick(n, candidates):
    for c in candidates:
        if c <= n and n % c == 0:
            return c
    return n


def _split_bf16(a):
    hi = a.astype(BF16)
    lo = (a - hi.astype(F32)).astype(BF16)
    return hi, lo


def _dot3(a, b, dims):
    ah, al = _split_bf16(a)
    bh, bl = _split_bf16(b)
    d = lambda x, y: lax.dot_general(x, y, (dims, ((), ())), preferred_element_type=F32)
    return d(ah, bh) + d(ah, bl) + d(al, bh)


_NN = ((1,), (0,))
_NT = ((1,), (1,))


def _rms(x, g):
    return x * lax.rsqrt(jnp.mean(x * x, axis=-1, keepdims=True) + EPS) * g


def _sigmoid(x):
    return 1.0 / (1.0 + jnp.exp(-x))


def _mod_kernel(c_ref, w_ref, b_ref, o_ref):
    c = c_ref[...]
    s = c * _sigmoid(c)
    o_ref[...] = _dot3(s, w_ref[...], _NN) + b_ref[...]


def _adaln(c, w_ada, b_ada):
    n, d = c.shape
    e = w_ada.shape[1]
    tn = _pick(e, (1024, 512, 256, 128))
    return pl.pallas_call(
        _mod_kernel,
        grid=(e // tn,),
        in_specs=[pl.BlockSpec((n, d), lambda j: (0, 0)),
                  pl.BlockSpec((d, tn), lambda j: (0, j)),
                  pl.BlockSpec((1, tn), lambda j: (0, j))],
        out_specs=pl.BlockSpec((n, tn), lambda j: (0, j)),
        out_shape=jax.ShapeDtypeStruct((n, e), F32),
        compiler_params=_cparams(("arbitrary",)),
    )(c, w_ada, b_ada.reshape(1, e))


def _rope_tables(pos):
    half = ROT_DIMS // 2
    inv_freq = ROPE_THETA ** (-2.0 * jnp.arange(half, dtype=F32) / ROT_DIMS)
    ang = pos.astype(F32)[:, None] * inv_freq[None, :]
    cos, sin = jnp.cos(ang), jnp.sin(ang)
    lane = jnp.arange(LANES) % HEAD_DIM
    idx = lane % half
    c = jnp.where(lane < ROT_DIMS, cos[:, idx], 1.0)
    sa = jnp.where(lane < half, -sin[:, idx], 0.0)
    sb = jnp.where((lane >= half) & (lane < ROT_DIMS), sin[:, idx], 0.0)
    return c.astype(F32), sa.astype(F32), sb.astype(F32)


def _rope(a, c, sa, sb):
    half = ROT_DIMS // 2
    out = []
    for j in range(a.shape[1] // LANES):
        x = a[:, j * LANES:(j + 1) * LANES]
        out.append(x * c + pltpu.roll(x, LANES - half, 1) * sa + pltpu.roll(x, half, 1) * sb)
    return jnp.concatenate(out, axis=1)


def _qkv_kernel(x_ref, sc_ref, sh_ref, g_ref, w_ref, c_ref, sa_ref, sb_ref, *outs, width, attn_layouts):
    x = x_ref[0]
    h = _rms(x, g_ref[...]) * (1.0 + sc_ref[0]) + sh_ref[0]
    proj = jnp.dot(h.astype(BF16), w_ref[...], preferred_element_type=F32)
    c, sa, sb = c_ref[...], sa_ref[...], sb_ref[...]
    w = width
    scale = HEAD_DIM ** -0.5 * (LOG2E if attn_layouts else 1.0)
    qd = _rope(proj[:, 0 * w:1 * w], c, sa, sb) * scale
    kd = _rope(proj[:, 1 * w:2 * w], c, sa, sb)
    vd = proj[:, 2 * w:3 * w]
    qm = _rope(proj[:, 3 * w:4 * w], c, sa, sb) * scale
    km = _rope(proj[:, 4 * w:5 * w], c, sa, sb)
    vm = proj[:, 5 * w:6 * w]
    if not attn_layouts:
        for r, v in zip(outs, (qd, kd, vd, qm, km, vm)):
            r[0] = v
        return
    kd_o, vd_o, km_o, vm_o, qm_o, qdT_o, qmT_o, kdh_o, kmh_o, vdT_o, vmT_o = outs
    ndh = w // (2 * HEAD_DIM)
    for h in range(ndh):
        kd_o[0, pl.ds(h, x.shape[0], stride=ndh), :] = kd[:, h * 2 * HEAD_DIM:(h + 1) * 2 * HEAD_DIM]
        vd_o[0, pl.ds(h, x.shape[0], stride=ndh), :] = vd[:, h * 2 * HEAD_DIM:(h + 1) * 2 * HEAD_DIM]
    km_o[0] = km
    vm_o[0] = vm
    qm_o[0] = qm
    qdT_o[0] = qd.T.astype(BF16)
    qmT_o[0] = qm.T.astype(BF16)
    for m in range(w // HEAD_DIM):
        kdh_o[0, m] = kd[:, m * HEAD_DIM:(m + 1) * HEAD_DIM].astype(BF16)
        kmh_o[0, m] = km[:, m * HEAD_DIM:(m + 1) * HEAD_DIM].astype(BF16)
    for r in range(x.shape[0] // MOBA_BLOCK):
        vdT_o[0, r] = vd[r * MOBA_BLOCK:(r + 1) * MOBA_BLOCK].T.astype(BF16)
        vmT_o[0, r] = vm[r * MOBA_BLOCK:(r + 1) * MOBA_BLOCK].T.astype(BF16)


def _qkv(x, sc, sh, g, w_in16, tabs, attn_layouts):
    b, s, d = x.shape
    w = w_in16.shape[1] // 6
    ts = _pick(s, (512, 256)) if attn_layouts else s
    sm = sc.shape[1]
    mod_spec = pl.BlockSpec((1, 1 if sm == 1 else ts, d), (lambda i, j: (i, 0, 0)) if sm == 1 else (lambda i, j: (i, j, 0)))
    tab_spec = pl.BlockSpec((ts, LANES), lambda i, j: (j, 0))
    row_spec = pl.BlockSpec((1, ts, w), lambda i, j: (i, j, 0))
    row_shape = jax.ShapeDtypeStruct((b, s, w), F32)
    if attn_layouts:
        nb = s // MOBA_BLOCK
        nh = w // HEAD_DIM
        ndh = w // (2 * HEAD_DIM)
        head_rows_spec = pl.BlockSpec((1, ts * ndh, 2 * HEAD_DIM), lambda i, j: (i, j, 0))
        head_rows_shape = jax.ShapeDtypeStruct((b, s * ndh, 2 * HEAD_DIM), F32)
        out_specs = [head_rows_spec] * 2 + [row_spec] * 3 + [
            pl.BlockSpec((1, w, ts), lambda i, j: (i, 0, j)),
            pl.BlockSpec((1, w, ts), lambda i, j: (i, 0, j)),
            pl.BlockSpec((1, nh, ts, HEAD_DIM), lambda i, j: (i, 0, j, 0)),
            pl.BlockSpec((1, nh, ts, HEAD_DIM), lambda i, j: (i, 0, j, 0)),
            pl.BlockSpec((1, ts // MOBA_BLOCK, w, MOBA_BLOCK), lambda i, j: (i, j, 0, 0)),
            pl.BlockSpec((1, ts // MOBA_BLOCK, w, MOBA_BLOCK), lambda i, j: (i, j, 0, 0)),
        ]
        out_shape = [head_rows_shape] * 2 + [row_shape] * 3 + [
            jax.ShapeDtypeStruct((b, w, s), BF16), jax.ShapeDtypeStruct((b, w, s), BF16),
            jax.ShapeDtypeStruct((b, nh, s, HEAD_DIM), BF16), jax.ShapeDtypeStruct((b, nh, s, HEAD_DIM), BF16),
            jax.ShapeDtypeStruct((b, nb, w, MOBA_BLOCK), BF16), jax.ShapeDtypeStruct((b, nb, w, MOBA_BLOCK), BF16),
        ]
    else:
        out_specs = [row_spec] * 6
        out_shape = [row_shape] * 6
    return pl.pallas_call(
        functools.partial(_qkv_kernel, width=w, attn_layouts=attn_layouts),
        grid=(b, s // ts),
        in_specs=[pl.BlockSpec((1, ts, d), lambda i, j: (i, j, 0)), mod_spec, mod_spec,
                  pl.BlockSpec((1, d), lambda i, j: (0, 0)),
                  pl.BlockSpec(w_in16.shape, lambda i, j: (0, 0)),
                  tab_spec, tab_spec, tab_spec],
        out_specs=out_specs,
        out_shape=out_shape,
        compiler_params=_cparams(("arbitrary", "arbitrary")),
    )(x, sc, sh, g.reshape(1, d), w_in16, *tabs)


def _select_kernel(q_ref, k_ref, o_ref, kmean_ref, gate_ref, *, nb, nh):
    j = pl.program_id(1)

    @pl.when(j == 0)
    def _():
        kmean_ref[...] = jnp.zeros_like(kmean_ref)

    kmean_ref[pl.ds(j, 1), :] = jnp.mean(k_ref[0], axis=0, keepdims=True)
    kmean = kmean_ref[...]
    lane_head = lax.broadcasted_iota(I32, kmean.shape, 1) // HEAD_DIM
    kbd = jnp.concatenate([jnp.where(lane_head == h, kmean, 0.0) for h in range(nh)], axis=0)
    gate = _dot3(kbd, q_ref[0], _NT).reshape(nh, nb, -1)
    gate_ref[...] = gate
    row = lax.broadcasted_iota(I32, gate.shape, 1)

    def count(jp, rank):
        other = gate_ref[:, pl.ds(jp, 1), :]
        beats = (other > gate) | ((other == gate) & (jp < row))
        return rank + jnp.where(beats, 1.0, 0.0)

    rank = lax.fori_loop(0, j, count, jnp.zeros(gate.shape, F32))
    o_ref[0] = jnp.where((row < j) & (rank < MOBA_TOPK), 1.0, 0.0)


def _moba_select(qm, km):
    b, s, w = qm.shape
    nb = s // MOBA_BLOCK
    nh = w // HEAD_DIM
    return pl.pallas_call(
        functools.partial(_select_kernel, nb=nb, nh=nh),
        grid=(b, nb),
        in_specs=[pl.BlockSpec((1, MOBA_BLOCK, w), lambda i, j: (i, j, 0)),
                  pl.BlockSpec((1, MOBA_BLOCK, w), lambda i, j: (i, j, 0))],
        out_specs=pl.BlockSpec((1, nh, nb, MOBA_BLOCK), lambda i, j: (i, 0, 0, j)),
        out_shape=jax.ShapeDtypeStruct((b, nh, nb, s), F32),
        scratch_shapes=[pltpu.VMEM((nb, w), F32), pltpu.VMEM((nh, nb, MOBA_BLOCK), F32)],
        compiler_params=_cparams(("arbitrary", "arbitrary")),
    )(qm, km)


def _flash_kernel(*refs, diff, lam_init):
    if diff:
        q_ref, k_ref, v_ref, lam_ref, g_ref, o_ref = refs
        sel_ref = None
    else:
        q_ref, k_ref, v_ref, sel_ref, o_ref = refs
    i = pl.program_id(2)
    tq = q_ref.shape[2]
    tk = MOBA_BLOCK
    nqb = tq // tk
    dv = v_ref.shape[2] if diff else HEAD_DIM
    qs = (q_ref[0, 0:HEAD_DIM, :], q_ref[0, HEAD_DIM:2 * HEAD_DIM, :])

    def step(jj, carry, diagonal):
        new = []
        for s in range(2):
            m, l, acc = carry[s]
            sts = []
            for h in range(nqb):
                blk = jj * nqb + h
                k = k_ref[0, s, pl.ds(pl.multiple_of(blk * tk, tk), tk), :]
                st = jnp.dot(k, qs[s], preferred_element_type=F32)
                if diagonal:
                    key = lax.broadcasted_iota(I32, st.shape, 0) + h * tk
                    qry = lax.broadcasted_iota(I32, st.shape, 1)
                    keep = key <= qry
                    if sel_ref is not None and h < nqb - 1:
                        keep = keep & ((qry < (h + 1) * tk) | (sel_ref[0, s, pl.ds(blk, 1), :] > 0.0))
                    st = jnp.where(keep, st, NEG)
                elif sel_ref is not None:
                    st = jnp.where(sel_ref[0, s, pl.ds(blk, 1), :] > 0.0, st, NEG)
                sts.append(st)
            m_new = m
            for st in sts:
                m_new = jnp.maximum(m_new, jnp.max(st, axis=0, keepdims=True))
            alpha = jnp.exp2(m - m_new)
            l = alpha * l
            acc = alpha * acc
            for h, st in enumerate(sts):
                blk = jj * nqb + h
                p = jnp.exp2(st - m_new)
                l = l + jnp.sum(p, axis=0, keepdims=True)
                v = v_ref[0, blk] if diff else v_ref[0, blk, s * HEAD_DIM:(s + 1) * HEAD_DIM, :]
                acc = acc + jnp.dot(v, p.astype(BF16), preferred_element_type=F32)
            new.append((m_new, l, acc))
        return tuple(new)

    init = tuple((jnp.full((1, tq), NEG, F32), jnp.zeros((1, tq), F32), jnp.zeros((dv, tq), F32)) for _ in range(2))
    carry = lax.fori_loop(0, i, lambda jj, c: step(jj, c, False), init)
    (m0, l0, a0), (m1, l1, a1) = step(i, carry, True)
    o0 = a0 / l0
    o1 = a1 / l1
    if diff:
        lv = lam_ref[...]
        lam = (jnp.exp(jnp.sum(lv[0:1] * lv[1:2], axis=1, keepdims=True))
               - jnp.exp(jnp.sum(lv[2:3] * lv[3:4], axis=1, keepdims=True)) + lam_init)
        o = o0 - lam * o1
        o = o * lax.rsqrt(jnp.mean(o * o, axis=0, keepdims=True) + EPS) * g_ref[...] * (1.0 - lam_init)
    else:
        o = jnp.concatenate([o0, o1], axis=0)
    o_ref[0] = o.T.astype(o_ref.dtype)


FLASH_TQ = 1024


def _flash(qT, k_hm, vT, extra, diff, lam_init):
    b, w, s = qT.shape
    nb = s // MOBA_BLOCK
    tq = FLASH_TQ if s % FLASH_TQ == 0 else MOBA_BLOCK
    ng = w // (2 * HEAD_DIM)
    in_specs = [pl.BlockSpec((1, 2 * HEAD_DIM, tq), lambda bi, g, i: (bi, g, i)),
                pl.BlockSpec((1, 2, s, HEAD_DIM), lambda bi, g, i: (bi, g, 0, 0)),
                pl.BlockSpec((1, nb, 2 * HEAD_DIM, MOBA_BLOCK), lambda bi, g, i: (bi, 0, g, 0))]
    if diff:
        lam_vecs, g_sub = extra
        in_specs += [pl.BlockSpec(lam_vecs.shape, lambda bi, g, i: (0, 0)),
                     pl.BlockSpec(g_sub.shape, lambda bi, g, i: (0, 0))]
        args = (lam_vecs, g_sub)
    else:
        (sel,) = extra
        in_specs += [pl.BlockSpec((1, 2, nb, tq), lambda bi, g, i: (bi, g, 0, i))]
        args = (sel,)
    return pl.pallas_call(
        functools.partial(_flash_kernel, diff=diff, lam_init=lam_init),
        grid=(b, ng, s // tq),
        in_specs=in_specs,
        out_specs=pl.BlockSpec((1, tq, 2 * HEAD_DIM), lambda bi, g, i: (bi, i, g)),
        out_shape=jax.ShapeDtypeStruct((b, s, w), BF16),
        compiler_params=_cparams(("arbitrary", "arbitrary", "arbitrary")),
    )(qT, k_hm, vT, *args)


def _post_kernel(od_ref, om_ref, x_ref, gt_ref, sc_ref, sh_ref, ga_ref, gf_ref, w_ref, wr_ref,
                 x1_ref, h2_ref, sc_out_ref):
    w = od_ref.shape[2]
    y = (jnp.dot(od_ref[0].astype(BF16), w_ref[0:w, :], preferred_element_type=F32)
         + jnp.dot(om_ref[0].astype(BF16), w_ref[w:2 * w, :], preferred_element_type=F32))
    x1 = x_ref[0] + gt_ref[0] * _rms(y, ga_ref[...])
    h2 = _rms(x1, gf_ref[...]) * (1.0 + sc_ref[0]) + sh_ref[0]
    x1_ref[0] = x1
    h2_ref[0] = h2
    sc_out_ref[...] = _sigmoid(_dot3(wr_ref[...], h2, _NT))


def _post_attn(od, om, x, gt, sc, sh, g_post, g_pre, w_out16, w_rT):
    b, s, d = x.shape
    w = od.shape[2]
    ne = w_rT.shape[0]
    ts = _pick(s, (512, 256, 128))
    sm = gt.shape[1]
    mod_spec = pl.BlockSpec((1, 1 if sm == 1 else ts, d), (lambda i, j: (i, 0, 0)) if sm == 1 else (lambda i, j: (i, j, 0)))
    vec_spec = pl.BlockSpec((1, d), lambda i, j: (0, 0))
    tok_spec = pl.BlockSpec((1, ts, d), lambda i, j: (i, j, 0))
    ns = s // ts
    return pl.pallas_call(
        _post_kernel,
        grid=(b, ns),
        in_specs=[pl.BlockSpec((1, ts, w), lambda i, j: (i, j, 0)), pl.BlockSpec((1, ts, w), lambda i, j: (i, j, 0)),
                  tok_spec, mod_spec, mod_spec, mod_spec, vec_spec, vec_spec,
                  pl.BlockSpec(w_out16.shape, lambda i, j: (0, 0)),
                  pl.BlockSpec(w_rT.shape, lambda i, j: (0, 0))],
        out_specs=[tok_spec, tok_spec, pl.BlockSpec((ne, ts), lambda i, j: (0, i * ns + j))],
        out_shape=[jax.ShapeDtypeStruct((b, s, d), F32), jax.ShapeDtypeStruct((b, s, d), F32),
                   jax.ShapeDtypeStruct((ne, b * s), F32)],
        compiler_params=_cparams(("arbitrary", "arbitrary")),
    )(od, om, x, gt, sc, sh, g_post.reshape(1, d), g_pre.reshape(1, d), w_out16, w_rT)


def _take_max(x, iota, fill):
    m = jnp.max(x, axis=0, keepdims=True)
    idx = jnp.min(jnp.where(x == m, iota, fill), axis=0, keepdims=True)
    hot = iota == idx
    return m, idx, hot, jnp.where(hot, -jnp.inf, x)


def _route_kernel(s_ref, b_ref, tri_ref, idx_ref, pos_ref, w_ref, cnt_ref, run_ref):
    step = pl.program_id(0)

    @pl.when(step == 0)
    def _():
        run_ref[...] = jnp.zeros_like(run_ref)

    scores = s_ref[...]
    ne, tn = scores.shape
    gs = ne // N_GROUPS
    biased = scores + b_ref[...]
    grp = biased.reshape(N_GROUPS, gs, tn)
    sub = lax.broadcasted_iota(I32, grp.shape, 1)
    m1 = jnp.max(grp, axis=1, keepdims=True)
    i1 = jnp.min(jnp.where(grp == m1, sub, gs), axis=1, keepdims=True)
    m2 = jnp.max(jnp.where(sub == i1, -jnp.inf, grp), axis=1, keepdims=True)
    gscore = (m1 + m2).reshape(N_GROUPS, tn)
    giota = lax.broadcasted_iota(I32, gscore.shape, 0)
    gmask = jnp.zeros(gscore.shape, jnp.bool_)
    for _ in range(TOPK_GROUPS):
        _, _, hot, gscore = _take_max(gscore, giota, N_GROUPS)
        gmask = gmask | hot
    emask = jnp.broadcast_to(gmask.reshape(N_GROUPS, 1, tn), (N_GROUPS, gs, tn)).reshape(ne, tn)
    cand = jnp.where(emask, biased, -jnp.inf)
    eiota = lax.broadcasted_iota(I32, cand.shape, 0)
    chosen = jnp.zeros(cand.shape, F32)
    idxs, wts = [], []
    for _ in range(TOP_K):
        _, idx, hot, cand = _take_max(cand, eiota, ne)
        idxs.append(idx)
        wts.append(jnp.sum(jnp.where(hot, scores, 0.0), axis=0, keepdims=True))
        chosen = chosen + jnp.where(hot, 1.0, 0.0)
    wsum = wts[0]
    for t in wts[1:]:
        wsum = wsum + t
    before = run_ref[...] + jnp.dot(chosen.astype(BF16), tri_ref[...], preferred_element_type=F32)
    poss = [jnp.sum(jnp.where(eiota == idx, before, 0.0), axis=0, keepdims=True) for idx in idxs]
    idx_ref[...] = jnp.concatenate(idxs, axis=0)
    pos_ref[...] = jnp.concatenate(poss, axis=0).astype(I32)
    w_ref[...] = jnp.concatenate(wts, axis=0) / wsum * ROUTED_SCALE
    run = run_ref[...] + jnp.sum(chosen, axis=1, keepdims=True)
    run_ref[...] = run
    cnt_ref[...] = jnp.broadcast_to(run, cnt_ref.shape)


def _route(scoresT, b_router):
    ne, n = scoresT.shape
    tn = _pick(n, (1152, 1024, 512, 384, 256, 128))
    tri = jnp.triu(jnp.ones((tn, tn), BF16), k=1)
    blk = pl.BlockSpec((TOP_K, tn), lambda i: (0, i))
    return pl.pallas_call(
        _route_kernel,
        grid=(n // tn,),
        in_specs=[pl.BlockSpec((ne, tn), lambda i: (0, i)),
                  pl.BlockSpec((ne, 1), lambda i: (0, 0)),
                  pl.BlockSpec((tn, tn), lambda i: (0, 0))],
        out_specs=[blk, blk, blk, pl.BlockSpec((ne, LANES), lambda i: (0, 0))],
        out_shape=[jax.ShapeDtypeStruct((TOP_K, n), I32), jax.ShapeDtypeStruct((TOP_K, n), I32),
                   jax.ShapeDtypeStruct((TOP_K, n), F32), jax.ShapeDtypeStruct((ne, LANES), F32)],
        scratch_shapes=[pltpu.VMEM((ne, 1), F32)],
        compiler_params=_cparams(("arbitrary",)),
    )(scoresT, b_router.reshape(ne, 1), tri)


def _row_copy(src, si, dst, di, sem):
    return pltpu.make_async_copy(src.at[pl.ds(si, 1)], dst.at[pl.ds(di, 1)], sem)


def _dispatch_kernel(dest_ref, h_ref, xs_ref, sem):
    tn = h_ref.shape[0]

    def issue(t, _):
        for k in range(TOP_K):
            _row_copy(h_ref, t, xs_ref, dest_ref[k, t], sem).start()
        return _

    lax.fori_loop(0, tn, issue, 0)

    def drain(t, _):
        for k in range(TOP_K):
            _row_copy(h_ref, t, xs_ref, dest_ref[k, t], sem).wait()
        return _

    lax.fori_loop(0, tn, drain, 0)


def _dispatch(dest, h2, n_rows):
    n, d = h2.shape
    tn = _pick(n, (384, 256, 128))
    return pl.pallas_call(
        _dispatch_kernel,
        grid=(n // tn,),
        in_specs=[pl.BlockSpec((TOP_K, tn), lambda i: (0, i), memory_space=pltpu.SMEM),
                  pl.BlockSpec((tn, d), lambda i: (i, 0))],
        out_specs=pl.BlockSpec(memory_space=pl.ANY),
        out_shape=jax.ShapeDtypeStruct((n_rows, d), F32),
        scratch_shapes=[pltpu.SemaphoreType.DMA],
        compiler_params=_cparams(("arbitrary",)),
    )(dest, h2)


def _expert_kernel(be_ref, nv_ref, x_ref, wgu_ref, wd_ref, o_ref):
    del be_ref
    de = wd_ref.shape[1]
    x = x_ref[...]
    x = jnp.where(lax.broadcasted_iota(I32, x.shape, 0) < nv_ref[pl.program_id(0)], x, 0.0)
    gu = jnp.dot(x.astype(BF16), wgu_ref[0], preferred_element_type=F32)
    g, u = gu[:, :de], gu[:, de:]
    mid = g * _sigmoid(g) * u
    o_ref[...] = jnp.dot(mid.astype(BF16), wd_ref[0], preferred_element_type=F32)


def _experts(blk_e, blk_valid, xs, wgu16, wd16):
    n_rows, d = xs.shape
    de = wd16.shape[1]
    return pl.pallas_call(
        _expert_kernel,
        grid_spec=pltpu.PrefetchScalarGridSpec(
            num_scalar_prefetch=2,
            grid=(n_rows // MOE_BLOCK,),
            in_specs=[pl.BlockSpec((MOE_BLOCK, d), lambda i, be, nv: (i, 0)),
                      pl.BlockSpec((1, d, 2 * de), lambda i, be, nv: (be[i], 0, 0)),
                      pl.BlockSpec((1, de, d), lambda i, be, nv: (be[i], 0, 0))],
            out_specs=pl.BlockSpec((MOE_BLOCK, d), lambda i, be, nv: (i, 0))),
        out_shape=jax.ShapeDtypeStruct((n_rows, d), F32),
        compiler_params=_cparams(("arbitrary",)),
    )(blk_e, blk_valid, xs, wgu16, wd16)


def _combine_kernel(dest_ref, ys_ref, w_ref, h_ref, x1_ref, gt_ref, g_ref, wsgu_ref, wsd_ref, o_ref, rows_ref, sem):
    tn = h_ref.shape[0]

    def issue(t, _):
        for k in range(TOP_K):
            _row_copy(ys_ref, dest_ref[k, t], rows_ref.at[k], t, sem).start()
        return _

    lax.fori_loop(0, tn, issue, 0)
    de = wsd_ref.shape[0]
    gu = jnp.dot(h_ref[...].astype(BF16), wsgu_ref[...], preferred_element_type=F32)
    g, u = gu[:, :de], gu[:, de:]
    f = jnp.dot((g * _sigmoid(g) * u).astype(BF16), wsd_ref[...], preferred_element_type=F32)

    def drain(t, _):
        for k in range(TOP_K):
            _row_copy(ys_ref, dest_ref[k, t], rows_ref.at[k], t, sem).wait()
        return _

    lax.fori_loop(0, tn, drain, 0)
    w = w_ref[...]
    for k in range(TOP_K):
        f = f + rows_ref[k] * w[:, k:k + 1]
    o_ref[0] = x1_ref[0] + gt_ref[0] * _rms(f, g_ref[...])


def _combine(dest, ys, wts, h2, x1, gt, g_post, wsgu16, wsd16, tok0):
    b, s, d = x1.shape
    tn = _pick(s, (256, 128))
    assert tok0 % tn == 0
    ns = s // tn
    off = tok0 // tn
    sm = gt.shape[1]
    mod_spec = pl.BlockSpec((1, 1 if sm == 1 else tn, d), (lambda i, j: (i, 0, 0)) if sm == 1 else (lambda i, j: (i, j, 0)))
    return pl.pallas_call(
        _combine_kernel,
        grid=(b, ns),
        in_specs=[pl.BlockSpec((TOP_K, tn), lambda i, j: (0, off + i * ns + j), memory_space=pltpu.SMEM),
                  pl.BlockSpec(memory_space=pl.ANY),
                  pl.BlockSpec((tn, TOP_K), lambda i, j: (off + i * ns + j, 0)),
                  pl.BlockSpec((tn, d), lambda i, j: (off + i * ns + j, 0)),
                  pl.BlockSpec((1, tn, d), lambda i, j: (i, j, 0)),
                  mod_spec,
                  pl.BlockSpec((1, d), lambda i, j: (0, 0)),
                  pl.BlockSpec(wsgu16.shape, lambda i, j: (0, 0)),
                  pl.BlockSpec(wsd16.shape, lambda i, j: (0, 0))],
        out_specs=pl.BlockSpec((1, tn, d), lambda i, j: (i, j, 0)),
        out_shape=jax.ShapeDtypeStruct((b, s, d), F32),
        scratch_shapes=[pltpu.VMEM((TOP_K, tn, d), F32), pltpu.SemaphoreType.DMA],
        compiler_params=_cparams(("arbitrary", "arbitrary")),
    )(dest, ys, wts, h2, x1, gt, g_post.reshape(1, d), wsgu16, wsd16)


PAGES_PER_STEP = 8
GATE_PAGES_PER_STEP = 16


def _per_map(x, nmaps):
    row_head = lax.broadcasted_iota(I32, (nmaps, x.shape[1]), 0) // 2
    out = jnp.zeros((nmaps, x.shape[1]), x.dtype)
    for h in range(x.shape[0]):
        out = jnp.where(row_head == h, x[h:h + 1, :], out)
    return out


def _dec_diff_kernel(pt_ref, q_ref, kn_ref, vn_ref, lam_ref, g_ref, *refs, lam_init, nheads):
    del pt_ref
    k_refs = refs[:PAGES_PER_STEP]
    v_refs = refs[PAGES_PER_STEP:2 * PAGES_PER_STEP]
    o_ref, m_ref, l_ref, acc_ref = refs[2 * PAGES_PER_STEP:]
    p = pl.program_id(1)
    nmaps = 2 * nheads
    dv = q_ref.shape[2]
    lane_map = lax.broadcasted_iota(I32, (nmaps, dv), 1) // HEAD_DIM
    row_map = lax.broadcasted_iota(I32, (nmaps, dv), 0) % 2
    q8 = jnp.where(lane_map == row_map, _per_map(q_ref[0], nmaps), 0.0)
    q16 = q8.astype(BF16)

    @pl.when(p == 0)
    def _():
        m_ref[...] = jnp.sum(q8 * _per_map(kn_ref[0], nmaps), axis=1, keepdims=True)
        l_ref[...] = jnp.ones_like(l_ref)
        acc_ref[...] = _per_map(vn_ref[0], nmaps)

    rows = k_refs[0].shape[1]
    own = (lax.broadcasted_iota(I32, (nmaps, rows), 1) % nheads) == (lax.broadcasted_iota(I32, (nmaps, rows), 0) // 2)
    st = jnp.concatenate(
        [jnp.where(own, lax.dot_general(q16, k_refs[r][0].astype(BF16), (_NT, ((), ())), preferred_element_type=F32), NEG)
         for r in range(PAGES_PER_STEP)], axis=1)
    m_old = m_ref[...]
    m_new = jnp.maximum(m_old, jnp.max(st, axis=1, keepdims=True))
    alpha = jnp.exp(m_old - m_new)
    pr = jnp.exp(st - m_new)
    l_ref[...] = alpha * l_ref[...] + jnp.sum(pr, axis=1, keepdims=True)
    pr16 = pr.astype(BF16)
    pv = jnp.zeros(acc_ref.shape, F32)
    for r in range(PAGES_PER_STEP):
        pv = pv + jnp.dot(pr16[:, r * rows:(r + 1) * rows], v_refs[r][0].astype(BF16), preferred_element_type=F32)
    acc_ref[...] = alpha * acc_ref[...] + pv
    m_ref[...] = m_new

    @pl.when(p == pl.num_programs(1) - 1)
    def _():
        o = acc_ref[...] / l_ref[...]
        lv = lam_ref[...]
        lam = (jnp.exp(jnp.sum(lv[0:1] * lv[1:2], axis=1, keepdims=True))
               - jnp.exp(jnp.sum(lv[2:3] * lv[3:4], axis=1, keepdims=True)) + lam_init)
        outs = [_rms(o[2 * h:2 * h + 1] - lam * o[2 * h + 1:2 * h + 2], g_ref[...]) * (1.0 - lam_init) for h in range(nheads)]
        o_ref[0] = jnp.concatenate(outs, axis=0)


def _dec_diff(page_table, q, k_new, v_new, cache_k, cache_v, lam_vecs, g_sub, lam_init):
    nseq, nheads, dv = q.shape
    npages = page_table.shape[1]
    rows = cache_k.shape[1]
    nsteps = npages // PAGES_PER_STEP

    def page_spec(r):
        return pl.BlockSpec((1, rows, dv), lambda b, p, pt: (pt[b, p * PAGES_PER_STEP + r], 0, 0))

    seq_spec = pl.BlockSpec((1, nheads, dv), lambda b, p, pt: (b, 0, 0))
    return pl.pallas_call(
        functools.partial(_dec_diff_kernel, lam_init=lam_init, nheads=nheads),
        grid_spec=pltpu.PrefetchScalarGridSpec(
            num_scalar_prefetch=1,
            grid=(nseq, nsteps),
            in_specs=[seq_spec, seq_spec, seq_spec,
                      pl.BlockSpec(lam_vecs.shape, lambda b, p, pt: (0, 0)),
                      pl.BlockSpec((1, dv), lambda b, p, pt: (0, 0))]
                     + [page_spec(r) for r in range(PAGES_PER_STEP)] * 2,
            out_specs=seq_spec,
            scratch_shapes=[pltpu.VMEM((2 * nheads, 1), F32), pltpu.VMEM((2 * nheads, 1), F32),
                            pltpu.VMEM((2 * nheads, dv), F32)]),
        out_shape=jax.ShapeDtypeStruct((nseq, nheads, dv), F32),
        compiler_params=_cparams(("arbitrary", "arbitrary")),
    )(page_table, q, k_new, v_new, lam_vecs, g_sub.reshape(1, dv),
      *([cache_k] * PAGES_PER_STEP), *([cache_v] * PAGES_PER_STEP))


def _dec_gate_kernel(pt_ref, q_ref, *refs, nh, pages_per_block, nblocks):
    del pt_ref
    pps = len(refs) - 2
    k_refs = refs[:pps]
    o_ref, gate_ref = refs[pps:]
    p = pl.program_id(1)

    @pl.when(p == 0)
    def _():
        gate_ref[...] = jnp.zeros_like(gate_ref)

    qb = q_ref[0]
    lane = lax.broadcasted_iota(I32, (nh, LANES), 1)
    g = gate_ref[...]
    for r in range(0, pps, pages_per_block):
        tot = k_refs[r][0]
        for e in range(1, pages_per_block):
            tot = tot + k_refs[r + e][0]
        prod = tot * qb
        per_head = jnp.concatenate(
            [jnp.sum(prod[h * HEAD_DIM:(h + 1) * HEAD_DIM], axis=0, keepdims=True) for h in range(nh)], axis=0)
        blk = (p * pps + r) // pages_per_block
        g = g + jnp.where(lane == blk, jnp.sum(per_head, axis=1, keepdims=True), 0.0)
    gate_ref[...] = g

    @pl.when(p == pl.num_programs(1) - 1)
    def _():
        gate = jnp.where(lane < nblocks, g, -jnp.inf)
        out = jnp.zeros((nh, LANES), I32)
        for r in range(MOBA_TOPK):
            m = jnp.max(gate, axis=1, keepdims=True)
            idx = jnp.min(jnp.where(gate == m, lane, LANES), axis=1, keepdims=True)
            out = jnp.where(lane == r, idx, out)
            gate = jnp.where(lane == idx, -jnp.inf, gate)
        o_ref[0] = out


def _dec_gate(page_table, q, cache_kT):
    nseq, w = q.shape
    npages = page_table.shape[1]
    page = cache_kT.shape[2]
    nh = w // HEAD_DIM
    ppb = MOBA_BLOCK // page
    nblocks = npages // ppb
    pps = _pick(npages, (GATE_PAGES_PER_STEP, PAGES_PER_STEP))
    assert nblocks <= LANES and pps % ppb == 0

    def page_spec(r):
        return pl.BlockSpec((1, w, page), lambda b, p, pt: (pt[b, p * pps + r], 0, 0))

    return pl.pallas_call(
        functools.partial(_dec_gate_kernel, nh=nh, pages_per_block=ppb, nblocks=nblocks),
        grid_spec=pltpu.PrefetchScalarGridSpec(
            num_scalar_prefetch=1,
            grid=(nseq, npages // pps),
            in_specs=[pl.BlockSpec((1, w, page), lambda b, p, pt: (b, 0, 0))]
                     + [page_spec(r) for r in range(pps)],
            out_specs=pl.BlockSpec((1, nh, LANES), lambda b, p, pt: (b, 0, 0)),
            scratch_shapes=[pltpu.VMEM((nh, LANES), F32)]),
        out_shape=jax.ShapeDtypeStruct((nseq, nh, LANES), I32),
        compiler_params=_cparams(("arbitrary", "arbitrary")),
    )(page_table, jnp.broadcast_to(q[:, :, None], (nseq, w, page)), *([cache_kT] * pps))


def _dec_moba_kernel(pt_ref, sel_ref, q_ref, kn_ref, vn_ref, *refs, npg):
    del pt_ref, sel_ref
    k_refs = refs[:npg]
    v_refs = refs[npg:2 * npg]
    o_ref = refs[-1]
    q = q_ref[0, 0]
    q8 = jnp.where(lax.broadcasted_iota(I32, (8, HEAD_DIM), 0) == 0, jnp.broadcast_to(q, (8, HEAD_DIM)), 0.0)
    q16 = q8.astype(BF16)
    st = jnp.concatenate([jnp.dot(q16, k_refs[r][0].astype(BF16), preferred_element_type=F32) for r in range(npg)],
                         axis=1)
    s_new = jnp.sum(q8 * kn_ref[0, 0], axis=1, keepdims=True)
    m = jnp.maximum(jnp.max(st, axis=1, keepdims=True), s_new)
    pr = jnp.exp(st - m)
    p_new = jnp.exp(s_new - m)
    l = jnp.sum(pr, axis=1, keepdims=True) + p_new
    pr16 = pr.astype(BF16)
    page = k_refs[0].shape[2]
    acc = p_new * vn_ref[0, 0]
    for r in range(npg):
        acc = acc + lax.dot_general(pr16[:, r * page:(r + 1) * page], v_refs[r][0].astype(BF16), (_NT, ((), ())),
                                    preferred_element_type=F32)
    o_ref[0, 0] = (acc / l)[0:1]


def _dec_moba(page_table, sel, q, k_new, v_new, cache_kT, cache_vT):
    nseq, nh = q.shape[:2]
    page = cache_kT.shape[2]
    ppb = MOBA_BLOCK // page
    npg = MOBA_TOPK * ppb

    def page_spec(r):
        def imap(b, h, pt, sl):
            blk = sl[b, h * MOBA_TOPK + r // ppb]
            return (pt[b, blk * ppb + r % ppb], h, 0)
        return pl.BlockSpec((1, HEAD_DIM, page), imap)

    seq_spec = pl.BlockSpec((1, 1, 1, HEAD_DIM), lambda b, h, pt, sl: (b, h, 0, 0))
    return pl.pallas_call(
        functools.partial(_dec_moba_kernel, npg=npg),
        grid_spec=pltpu.PrefetchScalarGridSpec(
            num_scalar_prefetch=2,
            grid=(nseq, nh),
            in_specs=[seq_spec, seq_spec, seq_spec] + [page_spec(r) for r in range(npg)] * 2,
            out_specs=seq_spec),
        out_shape=jax.ShapeDtypeStruct((nseq, nh, 1, HEAD_DIM), F32),
        compiler_params=_cparams(("arbitrary", "arbitrary")),
    )(page_table, sel, q, k_new, v_new, *([cache_kT] * npg), *([cache_vT] * npg))


def _layer(xp, xs, cp, cs, ckd, cvd, ckm, cvm, page_table, lam_init, w_ada, b_ada, g_attn_pre, g_attn_post,
           g_ffn_pre, g_ffn_post, w_in, w_out, lq1, lk1, lq2, lk2, g_subln, w_router, b_router,
           w_exp_gate, w_exp_up, w_exp_down, w_sh_gate, w_sh_up, w_sh_down):
    b, s, d = xp.shape
    nseq = xs.shape[0]
    past = page_table.shape[1] * ckd.shape[1]
    w = w_in.shape[1] // 6
    assert s % MOBA_BLOCK == 0 and MOBA_BLOCK % ckd.shape[1] == 0
    assert page_table.shape[1] % PAGES_PER_STEP == 0 and past // MOBA_BLOCK >= MOBA_TOPK and past % MOBA_BLOCK == 0
    assert xs.shape[1] == 1 and nseq % LANES == 0 and (b * s) % LANES == 0

    mod = _adaln(jnp.concatenate([cp, cs], axis=0), w_ada, b_ada)
    mod_p = [m[:, None, :] for m in jnp.split(mod[:b], 6, axis=-1)]
    mod_s = [m[None, :, :] for m in jnp.split(mod[b:], 6, axis=-1)]

    w_in16 = w_in.astype(BF16)
    w_out16 = w_out.astype(BF16)
    lam_vecs = jnp.stack([lq1, lk1, lq2, lk2]).astype(F32)
    g_col = g_subln.reshape(-1, 1)

    tabs_p = _rope_tables(jnp.arange(s, dtype=I32))
    kd, vd, km, vm, qm, qdT, qmT, kdh, kmh, vdT, vmT = _qkv(xp, mod_p[1], mod_p[0], g_attn_pre, w_in16, tabs_p, True)
    sel = _moba_select(qm, km)
    od = _flash(qdT, kdh, vdT, (lam_vecs, g_col), True, lam_init)
    om = _flash(qmT, kmh, vmT, (sel,), False, lam_init)
    x1_p, h2_p, sT_p = _post_attn(od, om, xp, mod_p[2], mod_p[4], mod_p[3], g_attn_post, g_ffn_pre, w_out16, w_router.T)

    tabs_s = _rope_tables(jnp.full((nseq,), past, dtype=I32))
    qd_s, kd_s, vd_s, qm_s, km_s, vm_s = _qkv(xs.reshape(1, nseq, d), mod_s[1], mod_s[0], g_attn_pre, w_in16, tabs_s, False)
    n_pool, page, ndh, ddh = ckd.shape
    nmh = ckm.shape[2]
    rows_view = lambda c: c.reshape(n_pool, page * ndh, ddh)
    cols_view = lambda c: jnp.transpose(c, (0, 2, 3, 1)).reshape(n_pool, nmh * HEAD_DIM, page)
    per_head = lambda a: a[0].reshape(nseq, ndh, ddh)
    per_mhead = lambda a: a[0].reshape(nseq, nmh, 1, HEAD_DIM)
    od_s = _dec_diff(page_table, per_head(qd_s), per_head(kd_s), per_head(vd_s), rows_view(ckd), rows_view(cvd),
                     lam_vecs, g_subln, lam_init)
    ckmT = cols_view(ckm)
    sel_s = _dec_gate(page_table, qm_s[0], ckmT)
    om_s = _dec_moba(page_table, sel_s[:, :, :MOBA_TOPK].reshape(nseq, -1), per_mhead(qm_s), per_mhead(km_s),
                     per_mhead(vm_s), ckmT, cols_view(cvm))
    x1_s, h2_s, sT_s = _post_attn(od_s.reshape(1, nseq, w), om_s.reshape(1, nseq, w), xs.reshape(1, nseq, d),
                                  mod_s[2], mod_s[4], mod_s[3], g_attn_post, g_ffn_pre, w_out16, w_router.T)

    n_all = b * s + nseq
    h2 = jnp.concatenate([h2_p.reshape(b * s, d), h2_s.reshape(nseq, d)], axis=0)
    idx, pos, wts, cnt = _route(jnp.concatenate([sT_p, sT_s], axis=1), b_router)
    ne = w_router.shape[1]
    counts = cnt[:, 0].astype(I32)
    padded = (counts + MOE_BLOCK - 1) // MOE_BLOCK * MOE_BLOCK
    pad_end = jnp.cumsum(padded)
    pad_start = pad_end - padded
    n_blk = n_all * TOP_K // MOE_BLOCK + ne
    e_iota = jnp.arange(ne, dtype=I32)
    blk_start = jnp.arange(n_blk, dtype=I32) * MOE_BLOCK
    blk_e = jnp.minimum(jnp.sum((pad_end[None, :] <= blk_start[:, None]).astype(I32), axis=1), ne - 1)
    blk_end = jnp.sum(jnp.where(blk_e[:, None] == e_iota[None, :], (pad_start + counts)[None, :], 0), axis=1)
    blk_valid = jnp.clip(blk_end - blk_start, 0, MOE_BLOCK).astype(I32)
    dest = pos + jnp.sum(jnp.where(idx[None] == e_iota[:, None, None], pad_start[:, None, None], 0), axis=0)
    xs_sorted = _dispatch(dest, h2, n_blk * MOE_BLOCK)
    wgu16 = jnp.concatenate([w_exp_gate, w_exp_up], axis=-1).astype(BF16)
    ys = _experts(blk_e, blk_valid, xs_sorted, wgu16, w_exp_down.astype(BF16))
    wsgu16 = jnp.concatenate([w_sh_gate, w_sh_up], axis=-1).astype(BF16)
    wsd16 = w_sh_down.astype(BF16)
    wts_t = wts.T
    y_p = _combine(dest, ys, wts_t, h2, x1_p, mod_p[5], g_ffn_post, wsgu16, wsd16, 0)
    y_s = _combine(dest, ys, wts_t, h2, x1_s, mod_s[5], g_ffn_post, wsgu16, wsd16, b * s)

    hd = ckd.shape[2:]
    hm = ckm.shape[2:]
    rows_p = (kd.reshape((b, s) + hd), vd.reshape((b, s) + hd), km.reshape((b, s) + hm), vm.reshape((b, s) + hm))
    rows_s = (kd_s.reshape((nseq, 1) + hd), vd_s.reshape((nseq, 1) + hd), km_s.reshape((nseq, 1) + hm), vm_s.reshape((nseq, 1) + hm))
    return y_p, y_s.reshape(nseq, 1, d), rows_p, rows_s


def kernel(x_prompt, x_sample, c_prompt, c_sample, cache_k_diff, cache_v_diff, cache_k_moba, cache_v_moba, page_table, w_ada, b_ada, g_attn_pre, g_attn_post, g_ffn_pre, g_ffn_post, w_in, w_out, lambda_q1, lambda_k1, lambda_q2, lambda_k2, g_subln, w_router, b_router, w_exp_gate, w_exp_up, w_exp_down, w_sh_gate, w_sh_up, w_sh_down):
    import math
    depth = w_ada.shape[0]
    yp, ys = x_prompt, x_sample
    rows_p, rows_s = [], []
    for l in range(depth):
        lam_init = 0.8 - 0.6 * math.exp(-0.3 * l)
        yp, ys, rp, rs = _layer(
            yp, ys, c_prompt, c_sample, cache_k_diff[l], cache_v_diff[l], cache_k_moba[l], cache_v_moba[l], page_table,
            lam_init, w_ada[l], b_ada[l], g_attn_pre[l], g_attn_post[l], g_ffn_pre[l], g_ffn_post[l], w_in[l], w_out[l],
            lambda_q1[l], lambda_k1[l], lambda_q2[l], lambda_k2[l], g_subln[l], w_router[l], b_router[l],
            w_exp_gate[l], w_exp_up[l], w_exp_down[l], w_sh_gate[l], w_sh_up[l], w_sh_down[l])
        rows_p.append(rp)
        rows_s.append(rs)
    stack = lambda rows, i: jnp.stack([r[i] for r in rows])
    return (yp, ys, stack(rows_p, 0), stack(rows_p, 1), stack(rows_p, 2), stack(rows_p, 3),
            stack(rows_s, 0), stack(rows_s, 1), stack(rows_s, 2), stack(rows_s, 3))
```

```python
import functools

import jax
import jax.numpy as jnp
from jax import lax
from jax.experimental import pallas as pl
from jax.experimental.pallas import tpu as pltpu

F32 = jnp.float32
BF16 = jnp.bfloat16
I32 = jnp.int32

HEAD_DIM = 64
ROT_DIMS = 16
ROPE_THETA = 500000.0
MOBA_BLOCK = 256
MOBA_TOPK = 3
N_GROUPS = 8
TOPK_GROUPS = 4
TOP_K = 8
ROUTED_SCALE = 2.5
MOE_BLOCK = 256
EPS = 1e-6
NEG = -1e30
LOG2E = 1.4426950408889634

LANES = 128
SUBLANES = 8
VMEM_LIMIT = 48 * 1024 * 1024


def _cparams(sem, vmem=VMEM_LIMIT):
    return pltpu.CompilerParams(dimension_semantics=sem, vmem_limit_bytes=vmem)


def _pick(n, candidates):
    for c in candidates:
        if c <= n and n % c == 0:
            return c
    return n


def _split_bf16(a):
    hi = a.astype(BF16)
    lo = (a - hi.astype(F32)).astype(BF16)
    return hi, lo


def _dot3(a, b, dims):
    ah, al = _split_bf16(a)
    bh, bl = _split_bf16(b)
    d = lambda x, y: lax.dot_general(x, y, (dims, ((), ())), preferred_element_type=F32)
    return d(ah, bh) + d(ah, bl) + d(al, bh)


_NN = ((1,), (0,))
_NT = ((1,), (1,))


def _rms(x, g):
    return x * lax.rsqrt(jnp.mean(x * x, axis=-1, keepdims=True) + EPS) * g


def _sigmoid(x):
    return 1.0 / (1.0 + jnp.exp(-x))


def _mod_kernel(c_ref, w_ref, b_ref, o_ref):
    c = c_ref[...]
    s = c * _sigmoid(c)
    o_ref[...] = _dot3(s, w_ref[...], _NN) + b_ref[...]


def _adaln(c, w_ada, b_ada):
    n, d = c.shape
    e = w_ada.shape[1]
    tn = _pick(e, (1024, 512, 256, 128))
    return pl.pallas_call(
        _mod_kernel,
        grid=(e // tn,),
        in_specs=[pl.BlockSpec((n, d), lambda j: (0, 0)),
                  pl.BlockSpec((d, tn), lambda j: (0, j)),
                  pl.BlockSpec((1, tn), lambda j: (0, j))],
        out_specs=pl.BlockSpec((n, tn), lambda j: (0, j)),
        out_shape=jax.ShapeDtypeStruct((n, e), F32),
        compiler_params=_cparams(("arbitrary",)),
    )(c, w_ada, b_ada.reshape(1, e))


def _rope_tables(pos):
    half = ROT_DIMS // 2
    inv_freq = ROPE_THETA ** (-2.0 * jnp.arange(half, dtype=F32) / ROT_DIMS)
    ang = pos.astype(F32)[:, None] * inv_freq[None, :]
    cos, sin = jnp.cos(ang), jnp.sin(ang)
    lane = jnp.arange(LANES) % HEAD_DIM
    idx = lane % half
    c = jnp.where(lane < ROT_DIMS, cos[:, idx], 1.0)
    sa = jnp.where(lane < half, -sin[:, idx], 0.0)
    sb = jnp.where((lane >= half) & (lane < ROT_DIMS), sin[:, idx], 0.0)
    return c.astype(F32), sa.astype(F32), sb.astype(F32)


def _rope(a, c, sa, sb):
    half = ROT_DIMS // 2
    out = []
    for j in range(a.shape[1] // LANES):
        x = a[:, j * LANES:(j + 1) * LANES]
        out.append(x * c + pltpu.roll(x, LANES - half, 1) * sa + pltpu.roll(x, half, 1) * sb)
    return jnp.concatenate(out, axis=1)


def _qkv_kernel(x_ref, sc_ref, sh_ref, g_ref, w_ref, c_ref, sa_ref, sb_ref, *outs, width, attn_layouts):
    x = x_ref[0]
    h = _rms(x, g_ref[...]) * (1.0 + sc_ref[0]) + sh_ref[0]
    proj = jnp.dot(h.astype(BF16), w_ref[...], preferred_element_type=F32)
    c, sa, sb = c_ref[...], sa_ref[...], sb_ref[...]
    w = width
    scale = HEAD_DIM ** -0.5 * (LOG2E if attn_layouts else 1.0)
    qd = _rope(proj[:, 0 * w:1 * w], c, sa, sb) * scale
    kd = _rope(proj[:, 1 * w:2 * w], c, sa, sb)
    vd = proj[:, 2 * w:3 * w]
    qm = _rope(proj[:, 3 * w:4 * w], c, sa, sb) * scale
    km = _rope(proj[:, 4 * w:5 * w], c, sa, sb)
    vm = proj[:, 5 * w:6 * w]
    if not attn_layouts:
        for r, v in zip(outs, (qd, kd, vd, qm, km, vm)):
            r[0] = v
        return
    kd_o, vd_o, km_o, vm_o, qm_o, qdT_o, qmT_o, kdh_o, kmh_o, vdT_o, vmT_o = outs
    ndh = w // (2 * HEAD_DIM)
    for h in range(ndh):
        kd_o[0, pl.ds(h, x.shape[0], stride=ndh), :] = kd[:, h * 2 * HEAD_DIM:(h + 1) * 2 * HEAD_DIM]
        vd_o[0, pl.ds(h, x.shape[0], stride=ndh), :] = vd[:, h * 2 * HEAD_DIM:(h + 1) * 2 * HEAD_DIM]
    km_o[0] = km
    vm_o[0] = vm
    qm_o[0] = qm
    qdT_o[0] = qd.T.astype(BF16)
    qmT_o[0] = qm.T.astype(BF16)
    for m in range(w // HEAD_DIM):
        kdh_o[0, m] = kd[:, m * HEAD_DIM:(m + 1) * HEAD_DIM].astype(BF16)
        kmh_o[0, m] = km[:, m * HEAD_DIM:(m + 1) * HEAD_DIM].astype(BF16)
    for r in range(x.shape[0] // MOBA_BLOCK):
        vdT_o[0, r] = vd[r * MOBA_BLOCK:(r + 1) * MOBA_BLOCK].T.astype(BF16)
        vmT_o[0, r] = vm[r * MOBA_BLOCK:(r + 1) * MOBA_BLOCK].T.astype(BF16)


def _qkv(x, sc, sh, g, w_in16, tabs, attn_layouts):
    b, s, d = x.shape
    w = w_in16.shape[1] // 6
    ts = _pick(s, (512, 256)) if attn_layouts else s
    sm = sc.shape[1]
    mod_spec = pl.BlockSpec((1, 1 if sm == 1 else ts, d), (lambda i, j: (i, 0, 0)) if sm == 1 else (lambda i, j: (i, j, 0)))
    tab_spec = pl.BlockSpec((ts, LANES), lambda i, j: (j, 0))
    row_spec = pl.BlockSpec((1, ts, w), lambda i, j: (i, j, 0))
    row_shape = jax.ShapeDtypeStruct((b, s, w), F32)
    if attn_layouts:
        nb = s // MOBA_BLOCK
        nh = w // HEAD_DIM
        ndh = w // (2 * HEAD_DIM)
        head_rows_spec = pl.BlockSpec((1, ts * ndh, 2 * HEAD_DIM), lambda i, j: (i, j, 0))
        head_rows_shape = jax.ShapeDtypeStruct((b, s * ndh, 2 * HEAD_DIM), F32)
        out_specs = [head_rows_spec] * 2 + [row_spec] * 3 + [
            pl.BlockSpec((1, w, ts), lambda i, j: (i, 0, j)),
            pl.BlockSpec((1, w, ts), lambda i, j: (i, 0, j)),
            pl.BlockSpec((1, nh, ts, HEAD_DIM), lambda i, j: (i, 0, j, 0)),
            pl.BlockSpec((1, nh, ts, HEAD_DIM), lambda i, j: (i, 0, j, 0)),
            pl.BlockSpec((1, ts // MOBA_BLOCK, w, MOBA_BLOCK), lambda i, j: (i, j, 0, 0)),
            pl.BlockSpec((1, ts // MOBA_BLOCK, w, MOBA_BLOCK), lambda i, j: (i, j, 0, 0)),
        ]
        out_shape = [head_rows_shape] * 2 + [row_shape] * 3 + [
            jax.ShapeDtypeStruct((b, w, s), BF16), jax.ShapeDtypeStruct((b, w, s), BF16),
            jax.ShapeDtypeStruct((b, nh, s, HEAD_DIM), BF16), jax.ShapeDtypeStruct((b, nh, s, HEAD_DIM), BF16),
            jax.ShapeDtypeStruct((b, nb, w, MOBA_BLOCK), BF16), jax.ShapeDtypeStruct((b, nb, w, MOBA_BLOCK), BF16),
        ]
    else:
        out_specs = [row_spec] * 6
        out_shape = [row_shape] * 6
    return pl.pallas_call(
        functools.partial(_qkv_kernel, width=w, attn_layouts=attn_layouts),
        grid=(b, s // ts),
        in_specs=[pl.BlockSpec((1, ts, d), lambda i, j: (i, j, 0)), mod_spec, mod_spec,
                  pl.BlockSpec((1, d), lambda i, j: (0, 0)),
                  pl.BlockSpec(w_in16.shape, lambda i, j: (0, 0)),
                  tab_spec, tab_spec, tab_spec],
        out_specs=out_specs,
        out_shape=out_shape,
        compiler_params=_cparams(("arbitrary", "arbitrary")),
    )(x, sc, sh, g.reshape(1, d), w_in16, *tabs)


def _select_kernel(q_ref, k_ref, o_ref, kmean_ref, gate_ref, *, nb, nh):
    j = pl.program_id(1)

    @pl.when(j == 0)
    def _():
        kmean_ref[...] = jnp.zeros_like(kmean_ref)

    kmean_ref[pl.ds(j, 1), :] = jnp.mean(k_ref[0], axis=0, keepdims=True)
    kmean = kmean_ref[...]
    lane_head = lax.broadcasted_iota(I32, kmean.shape, 1) // HEAD_DIM
    kbd = jnp.concatenate([jnp.where(lane_head == h, kmean, 0.0) for h in range(nh)], axis=0)
    gate = _dot3(kbd, q_ref[0], _NT).reshape(nh, nb, -1)
    gate_ref[...] = gate
    row = lax.broadcasted_iota(I32, gate.shape, 1)

    def count(jp, rank):
        other = gate_ref[:, pl.ds(jp, 1), :]
        beats = (other > gate) | ((other == gate) & (jp < row))
        return rank + jnp.where(beats, 1.0, 0.0)

    rank = lax.fori_loop(0, j, count, jnp.zeros(gate.shape, F32))
    o_ref[0] = jnp.where((row < j) & (rank < MOBA_TOPK), 1.0, 0.0)


def _moba_select(qm, km):
    b, s, w = qm.shape
    nb = s // MOBA_BLOCK
    nh = w // HEAD_DIM
    return pl.pallas_call(
        functools.partial(_select_kernel, nb=nb, nh=nh),
        grid=(b, nb),
        in_specs=[pl.BlockSpec((1, MOBA_BLOCK, w), lambda i, j: (i, j, 0)),
                  pl.BlockSpec((1, MOBA_BLOCK, w), lambda i, j: (i, j, 0))],
        out_specs=pl.BlockSpec((1, nh, nb, MOBA_BLOCK), lambda i, j: (i, 0, 0, j)),
        out_shape=jax.ShapeDtypeStruct((b, nh, nb, s), F32),
        scratch_shapes=[pltpu.VMEM((nb, w), F32), pltpu.VMEM((nh, nb, MOBA_BLOCK), F32)],
        compiler_params=_cparams(("arbitrary", "arbitrary")),
    )(qm, km)


def _flash_kernel(*refs, diff, lam_init):
    if diff:
        q_ref, k_ref, v_ref, lam_ref, g_ref, o_ref = refs
        sel_ref = None
    else:
        q_ref, k_ref, v_ref, sel_ref, o_ref = refs
    i = pl.program_id(2)
    tq = q_ref.shape[2]
    tk = MOBA_BLOCK
    nqb = tq // tk
    dv = v_ref.shape[2] if diff else HEAD_DIM
    qs = (q_ref[0, 0:HEAD_DIM, :], q_ref[0, HEAD_DIM:2 * HEAD_DIM, :])

    def step(jj, carry, diagonal):
        new = []
        for s in range(2):
            m, l, acc = carry[s]
            sts = []
            for h in range(nqb):
                blk = jj * nqb + h
                k = k_ref[0, s, pl.ds(pl.multiple_of(blk * tk, tk), tk), :]
                st = jnp.dot(k, qs[s], preferred_element_type=F32)
                if diagonal:
                    key = lax.broadcasted_iota(I32, st.shape, 0) + h * tk
                    qry = lax.broadcasted_iota(I32, st.shape, 1)
                    keep = key <= qry
                    if sel_ref is not None and h < nqb - 1:
                        keep = keep & ((qry < (h + 1) * tk) | (sel_ref[0, s, pl.ds(blk, 1), :] > 0.0))
                    st = jnp.where(keep, st, NEG)
                elif sel_ref is not None:
                    st = jnp.where(sel_ref[0, s, pl.ds(blk, 1), :] > 0.0, st, NEG)
                sts.append(st)
            m_new = m
            for st in sts:
                m_new = jnp.maximum(m_new, jnp.max(st, axis=0, keepdims=True))
            alpha = jnp.exp2(m - m_new)
            l = alpha * l
            acc = alpha * acc
            for h, st in enumerate(sts):
                blk = jj * nqb + h
                p = jnp.exp2(st - m_new)
                l = l + jnp.sum(p, axis=0, keepdims=True)
                v = v_ref[0, blk] if diff else v_ref[0, blk, s * HEAD_DIM:(s + 1) * HEAD_DIM, :]
                acc = acc + jnp.dot(v, p.astype(BF16), preferred_element_type=F32)
            new.append((m_new, l, acc))
        return tuple(new)

    init = tuple((jnp.full((1, tq), NEG, F32), jnp.zeros((1, tq), F32), jnp.zeros((dv, tq), F32)) for _ in range(2))
    carry = lax.fori_loop(0, i, lambda jj, c: step(jj, c, False), init)
    (m0, l0, a0), (m1, l1, a1) = step(i, carry, True)
    o0 = a0 / l0
    o1 = a1 / l1
    if diff:
        lv = lam_ref[...]
        lam = (jnp.exp(jnp.sum(lv[0:1] * lv[1:2], axis=1, keepdims=True))
               - jnp.exp(jnp.sum(lv[2:3] * lv[3:4], axis=1, keepdims=True)) + lam_init)
        o = o0 - lam * o1
        o = o * lax.rsqrt(jnp.mean(o * o, axis=0, keepdims=True) + EPS) * g_ref[...] * (1.0 - lam_init)
    else:
        o = jnp.concatenate([o0, o1], axis=0)
    o_ref[0] = o.T.astype(o_ref.dtype)


FLASH_TQ = 1024


def _flash(qT, k_hm, vT, extra, diff, lam_init):
    b, w, s = qT.shape
    nb = s // MOBA_BLOCK
    tq = FLASH_TQ if s % FLASH_TQ == 0 else MOBA_BLOCK
    ng = w // (2 * HEAD_DIM)
    in_specs = [pl.BlockSpec((1, 2 * HEAD_DIM, tq), lambda bi, g, i: (bi, g, i)),
                pl.BlockSpec((1, 2, s, HEAD_DIM), lambda bi, g, i: (bi, g, 0, 0)),
                pl.BlockSpec((1, nb, 2 * HEAD_DIM, MOBA_BLOCK), lambda bi, g, i: (bi, 0, g, 0))]
    if diff:
        lam_vecs, g_sub = extra
        in_specs += [pl.BlockSpec(lam_vecs.shape, lambda bi, g, i: (0, 0)),
                     pl.BlockSpec(g_sub.shape, lambda bi, g, i: (0, 0))]
        args = (lam_vecs, g_sub)
    else:
        (sel,) = extra
        in_specs += [pl.BlockSpec((1, 2, nb, tq), lambda bi, g, i: (bi, g, 0, i))]
        args = (sel,)
    return pl.pallas_call(
        functools.partial(_flash_kernel, diff=diff, lam_init=lam_init),
        grid=(b, ng, s // tq),
        in_specs=in_specs,
        out_specs=pl.BlockSpec((1, tq, 2 * HEAD_DIM), lambda bi, g, i: (bi, i, g)),
        out_shape=jax.ShapeDtypeStruct((b, s, w), BF16),
        compiler_params=_cparams(("arbitrary", "arbitrary", "arbitrary")),
    )(qT, k_hm, vT, *args)


def _post_kernel(od_ref, om_ref, x_ref, gt_ref, sc_ref, sh_ref, ga_ref, gf_ref, w_ref, wr_ref,
                 x1_ref, h2_ref, sc_out_ref):
    w = od_ref.shape[2]
    y = (jnp.dot(od_ref[0].astype(BF16), w_ref[0:w, :], preferred_element_type=F32)
         + jnp.dot(om_ref[0].astype(BF16), w_ref[w:2 * w, :], preferred_element_type=F32))
    x1 = x_ref[0] + gt_ref[0] * _rms(y, ga_ref[...])
    h2 = _rms(x1, gf_ref[...]) * (1.0 + sc_ref[0]) + sh_ref[0]
    x1_ref[0] = x1
    h2_ref[0] = h2
    sc_out_ref[...] = _sigmoid(_dot3(wr_ref[...], h2, _NT))


def _post_attn(od, om, x, gt, sc, sh, g_post, g_pre, w_out16, w_rT):
    b, s, d = x.shape
    w = od.shape[2]
    ne = w_rT.shape[0]
    ts = _pick(s, (512, 256, 128))
    sm = gt.shape[1]
    mod_spec = pl.BlockSpec((1, 1 if sm == 1 else ts, d), (lambda i, j: (i, 0, 0)) if sm == 1 else (lambda i, j: (i, j, 0)))
    vec_spec = pl.BlockSpec((1, d), lambda i, j: (0, 0))
    tok_spec = pl.BlockSpec((1, ts, d), lambda i, j: (i, j, 0))
    ns = s // ts
    return pl.pallas_call(
        _post_kernel,
        grid=(b, ns),
        in_specs=[pl.BlockSpec((1, ts, w), lambda i, j: (i, j, 0)), pl.BlockSpec((1, ts, w), lambda i, j: (i, j, 0)),
                  tok_spec, mod_spec, mod_spec, mod_spec, vec_spec, vec_spec,
                  pl.BlockSpec(w_out16.shape, lambda i, j: (0, 0)),
                  pl.BlockSpec(w_rT.shape, lambda i, j: (0, 0))],
        out_specs=[tok_spec, tok_spec, pl.BlockSpec((ne, ts), lambda i, j: (0, i * ns + j))],
        out_shape=[jax.ShapeDtypeStruct((b, s, d), F32), jax.ShapeDtypeStruct((b, s, d), F32),
                   jax.ShapeDtypeStruct((ne, b * s), F32)],
        compiler_params=_cparams(("arbitrary", "arbitrary")),
    )(od, om, x, gt, sc, sh, g_post.reshape(1, d), g_pre.reshape(1, d), w_out16, w_rT)


def _take_max(x, iota, fill):
    m = jnp.max(x, axis=0, keepdims=True)
    idx = jnp.min(jnp.where(x == m, iota, fill), axis=0, keepdims=True)
    hot = iota == idx
    return m, idx, hot, jnp.where(hot, -jnp.inf, x)


def _route_kernel(s_ref, b_ref, tri_ref, idx_ref, pos_ref, w_ref, cnt_ref, run_ref):
    step = pl.program_id(0)

    @pl.when(step == 0)
    def _():
        run_ref[...] = jnp.zeros_like(run_ref)

    scores = s_ref[...]
    ne, tn = scores.shape
    gs = ne // N_GROUPS
    biased = scores + b_ref[...]
    grp = biased.reshape(N_GROUPS, gs, tn)
    sub = lax.broadcasted_iota(I32, grp.shape, 1)
    m1 = jnp.max(grp, axis=1, keepdims=True)
    i1 = jnp.min(jnp.where(grp == m1, sub, gs), axis=1, keepdims=True)
    m2 = jnp.max(jnp.where(sub == i1, -jnp.inf, grp), axis=1, keepdims=True)
    gscore = (m1 + m2).reshape(N_GROUPS, tn)
    giota = lax.broadcasted_iota(I32, gscore.shape, 0)
    gmask = jnp.zeros(gscore.shape, jnp.bool_)
    for _ in range(TOPK_GROUPS):
        _, _, hot, gscore = _take_max(gscore, giota, N_GROUPS)
        gmask = gmask | hot
    emask = jnp.broadcast_to(gmask.reshape(N_GROUPS, 1, tn), (N_GROUPS, gs, tn)).reshape(ne, tn)
    cand = jnp.where(emask, biased, -jnp.inf)
    eiota = lax.broadcasted_iota(I32, cand.shape, 0)
    chosen = jnp.zeros(cand.shape, F32)
    idxs, wts = [], []
    for _ in range(TOP_K):
        _, idx, hot, cand = _take_max(cand, eiota, ne)
        idxs.append(idx)
        wts.append(jnp.sum(jnp.where(hot, scores, 0.0), axis=0, keepdims=True))
        chosen = chosen + jnp.where(hot, 1.0, 0.0)
    wsum = wts[0]
    for t in wts[1:]:
        wsum = wsum + t
    before = run_ref[...] + jnp.dot(chosen.astype(BF16), tri_ref[...], preferred_element_type=F32)
    poss = [jnp.sum(jnp.where(eiota == idx, before, 0.0), axis=0, keepdims=True) for idx in idxs]
    idx_ref[...] = jnp.concatenate(idxs, axis=0)
    pos_ref[...] = jnp.concatenate(poss, axis=0).astype(I32)
    w_ref[...] = jnp.concatenate(wts, axis=0) / wsum * ROUTED_SCALE
    run = run_ref[...] + jnp.sum(chosen, axis=1, keepdims=True)
    run_ref[...] = run
    cnt_ref[...] = jnp.broadcast_to(run, cnt_ref.shape)


def _route(scoresT, b_router):
    ne, n = scoresT.shape
    tn = _pick(n, (1152, 1024, 512, 384, 256, 128))
    tri = jnp.triu(jnp.ones((tn, tn), BF16), k=1)
    blk = pl.BlockSpec((TOP_K, tn), lambda i: (0, i))
    return pl.pallas_call(
        _route_kernel,
        grid=(n // tn,),
        in_specs=[pl.BlockSpec((ne, tn), lambda i: (0, i)),
                  pl.BlockSpec((ne, 1), lambda i: (0, 0)),
                  pl.BlockSpec((tn, tn), lambda i: (0, 0))],
        out_specs=[blk, blk, blk, pl.BlockSpec((ne, LANES), lambda i: (0, 0))],
        out_shape=[jax.ShapeDtypeStruct((TOP_K, n), I32), jax.ShapeDtypeStruct((TOP_K, n), I32),
                   jax.ShapeDtypeStruct((TOP_K, n), F32), jax.ShapeDtypeStruct((ne, LANES), F32)],
        scratch_shapes=[pltpu.VMEM((ne, 1), F32)],
        compiler_params=_cparams(("arbitrary",)),
    )(scoresT, b_router.reshape(ne, 1), tri)


def _row_copy(src, si, dst, di, sem):
    return pltpu.make_async_copy(src.at[pl.ds(si, 1)], dst.at[pl.ds(di, 1)], sem)


def _dispatch_kernel(dest_ref, h_ref, xs_ref, sem):
    tn = h_ref.shape[0]

    def issue(t, _):
        for k in range(TOP_K):
            _row_copy(h_ref, t, xs_ref, dest_ref[k, t], sem).start()
        return _

    lax.fori_loop(0, tn, issue, 0)

    def drain(t, _):
        for k in range(TOP_K):
            _row_copy(h_ref, t, xs_ref, dest_ref[k, t], sem).wait()
        return _

    lax.fori_loop(0, tn, drain, 0)


def _dispatch(dest, h2, n_rows):
    n, d = h2.shape
    tn = _pick(n, (128,))
    return pl.pallas_call(
        _dispatch_kernel,
        grid=(n // tn,),
        in_specs=[pl.BlockSpec((TOP_K, tn), lambda i: (0, i), memory_space=pltpu.SMEM),
                  pl.BlockSpec((tn, d), lambda i: (i, 0))],
        out_specs=pl.BlockSpec(memory_space=pl.ANY),
        out_shape=jax.ShapeDtypeStruct((n_rows, d), F32),
        scratch_shapes=[pltpu.SemaphoreType.DMA],
        compiler_params=_cparams(("arbitrary",)),
    )(dest, h2)


def _expert_kernel(be_ref, nv_ref, x_ref, wgu_ref, wd_ref, o_ref):
    del be_ref
    de = wd_ref.shape[1]
    x = x_ref[...]
    x = jnp.where(lax.broadcasted_iota(I32, x.shape, 0) < nv_ref[pl.program_id(0)], x, 0.0)
    gu = jnp.dot(x.astype(BF16), wgu_ref[0], preferred_element_type=F32)
    g, u = gu[:, :de], gu[:, de:]
    mid = g * _sigmoid(g) * u
    y = jnp.dot(mid.astype(BF16), wd_ref[0], preferred_element_type=F32)
    for j in range(SUBLANES):
        o_ref[pl.ds(j, x.shape[0], stride=SUBLANES), :] = y[:, j * LANES:(j + 1) * LANES]


def _experts(blk_e, blk_valid, xs, wgu16, wd16):
    n_rows, d = xs.shape
    de = wd16.shape[1]
    return pl.pallas_call(
        _expert_kernel,
        grid_spec=pltpu.PrefetchScalarGridSpec(
            num_scalar_prefetch=2,
            grid=(n_rows // MOE_BLOCK,),
            in_specs=[pl.BlockSpec((MOE_BLOCK, d), lambda i, be, nv: (i, 0)),
                      pl.BlockSpec((1, d, 2 * de), lambda i, be, nv: (be[i], 0, 0)),
                      pl.BlockSpec((1, de, d), lambda i, be, nv: (be[i], 0, 0))],
            out_specs=pl.BlockSpec((MOE_BLOCK * SUBLANES, LANES), lambda i, be, nv: (i, 0))),
        out_shape=jax.ShapeDtypeStruct((n_rows * SUBLANES, LANES), F32),
        compiler_params=_cparams(("arbitrary",)),
    )(blk_e, blk_valid, xs, wgu16, wd16)


def _combine_kernel(dest_ref, ys_ref, w_ref, h_ref, x1_ref, gt_ref, g_ref, wsgu_ref, wsd_ref, o_ref, rows_ref, sem):
    tn = h_ref.shape[0]

    def tile_copy(k, t):
        src = pl.multiple_of(dest_ref[k, t] * SUBLANES, SUBLANES)
        dst = pl.multiple_of(t * SUBLANES, SUBLANES)
        return pltpu.make_async_copy(ys_ref.at[pl.ds(src, SUBLANES)], rows_ref.at[k, pl.ds(dst, SUBLANES)], sem)

    def issue(t, _):
        for k in range(TOP_K):
            tile_copy(k, t).start()
        return _

    lax.fori_loop(0, tn, issue, 0)
    de = wsd_ref.shape[0]
    gu = jnp.dot(h_ref[...].astype(BF16), wsgu_ref[...], preferred_element_type=F32)
    g, u = gu[:, :de], gu[:, de:]
    f = jnp.dot((g * _sigmoid(g) * u).astype(BF16), wsd_ref[...], preferred_element_type=F32)

    def drain(t, _):
        for k in range(TOP_K):
            tile_copy(k, t).wait()
        return _

    lax.fori_loop(0, tn, drain, 0)
    w = w_ref[...]
    parts = []
    for j in range(SUBLANES):
        fj = f[:, j * LANES:(j + 1) * LANES]
        for k in range(TOP_K):
            fj = fj + rows_ref[k, pl.ds(j, tn, stride=SUBLANES), :] * w[:, k:k + 1]
        parts.append(fj)
    o_ref[0] = x1_ref[0] + gt_ref[0] * _rms(jnp.concatenate(parts, axis=1), g_ref[...])


def _combine(dest, ys, wts, h2, x1, gt, g_post, wsgu16, wsd16, tok0):
    b, s, d = x1.shape
    tn = _pick(s, (256, 128))
    assert tok0 % tn == 0 and d == SUBLANES * LANES
    ns = s // tn
    off = tok0 // tn
    sm = gt.shape[1]
    mod_spec = pl.BlockSpec((1, 1 if sm == 1 else tn, d), (lambda i, j: (i, 0, 0)) if sm == 1 else (lambda i, j: (i, j, 0)))
    return pl.pallas_call(
        _combine_kernel,
        grid=(b, ns),
        in_specs=[pl.BlockSpec((TOP_K, tn), lambda i, j: (0, off + i * ns + j), memory_space=pltpu.SMEM),
                  pl.BlockSpec(memory_space=pl.ANY),
                  pl.BlockSpec((tn, TOP_K), lambda i, j: (off + i * ns + j, 0)),
                  pl.BlockSpec((tn, d), lambda i, j: (off + i * ns + j, 0)),
                  pl.BlockSpec((1, tn, d), lambda i, j: (i, j, 0)),
                  mod_spec,
                  pl.BlockSpec((1, d), lambda i, j: (0, 0)),
                  pl.BlockSpec(wsgu16.shape, lambda i, j: (0, 0)),
                  pl.BlockSpec(wsd16.shape, lambda i, j: (0, 0))],
        out_specs=pl.BlockSpec((1, tn, d), lambda i, j: (i, j, 0)),
        out_shape=jax.ShapeDtypeStruct((b, s, d), F32),
        scratch_shapes=[pltpu.VMEM((TOP_K, tn * SUBLANES, LANES), F32), pltpu.SemaphoreType.DMA],
        compiler_params=_cparams(("arbitrary", "arbitrary")),
    )(dest, ys, wts, h2, x1, gt, g_post.reshape(1, d), wsgu16, wsd16)


PAGES_PER_STEP = 8
GATE_PAGES_PER_STEP = 16


def _per_map(x, nmaps):
    row_head = lax.broadcasted_iota(I32, (nmaps, x.shape[1]), 0) // 2
    out = jnp.zeros((nmaps, x.shape[1]), x.dtype)
    for h in range(x.shape[0]):
        out = jnp.where(row_head == h, x[h:h + 1, :], out)
    return out


def _dec_diff_kernel(pt_ref, q_ref, kn_ref, vn_ref, lam_ref, g_ref, *refs, lam_init, nheads):
    del pt_ref
    k_refs = refs[:PAGES_PER_STEP]
    v_refs = refs[PAGES_PER_STEP:2 * PAGES_PER_STEP]
    o_ref, m_ref, l_ref, acc_ref = refs[2 * PAGES_PER_STEP:]
    p = pl.program_id(1)
    nmaps = 2 * nheads
    dv = q_ref.shape[2]
    lane_map = lax.broadcasted_iota(I32, (nmaps, dv), 1) // HEAD_DIM
    row_map = lax.broadcasted_iota(I32, (nmaps, dv), 0) % 2
    q8 = jnp.where(lane_map == row_map, _per_map(q_ref[0], nmaps), 0.0)
    q16 = q8.astype(BF16)

    @pl.when(p == 0)
    def _():
        m_ref[...] = jnp.sum(q8 * _per_map(kn_ref[0], nmaps), axis=1, keepdims=True)
        l_ref[...] = jnp.ones_like(l_ref)
        acc_ref[...] = _per_map(vn_ref[0], nmaps)

    rows = k_refs[0].shape[1]
    own = (lax.broadcasted_iota(I32, (nmaps, rows), 1) % nheads) == (lax.broadcasted_iota(I32, (nmaps, rows), 0) // 2)
    st = jnp.concatenate(
        [jnp.where(own, lax.dot_general(q16, k_refs[r][0].astype(BF16), (_NT, ((), ())), preferred_element_type=F32), NEG)
         for r in range(PAGES_PER_STEP)], axis=1)
    m_old = m_ref[...]
    m_new = jnp.maximum(m_old, jnp.max(st, axis=1, keepdims=True))
    alpha = jnp.exp(m_old - m_new)
    pr = jnp.exp(st - m_new)
    l_ref[...] = alpha * l_ref[...] + jnp.sum(pr, axis=1, keepdims=True)
    pr16 = pr.astype(BF16)
    pv = jnp.zeros(acc_ref.shape, F32)
    for r in range(PAGES_PER_STEP):
        pv = pv + jnp.dot(pr16[:, r * rows:(r + 1) * rows], v_refs[r][0].astype(BF16), preferred_element_type=F32)
    acc_ref[...] = alpha * acc_ref[...] + pv
    m_ref[...] = m_new

    @pl.when(p == pl.num_programs(1) - 1)
    def _():
        o = acc_ref[...] / l_ref[...]
        lv = lam_ref[...]
        lam = (jnp.exp(jnp.sum(lv[0:1] * lv[1:2], axis=1, keepdims=True))
               - jnp.exp(jnp.sum(lv[2:3] * lv[3:4], axis=1, keepdims=True)) + lam_init)
        outs = [_rms(o[2 * h:2 * h + 1] - lam * o[2 * h + 1:2 * h + 2], g_ref[...]) * (1.0 - lam_init) for h in range(nheads)]
        o_ref[0] = jnp.concatenate(outs, axis=0)


def _dec_diff(page_table, q, k_new, v_new, cache_k, cache_v, lam_vecs, g_sub, lam_init):
    nseq, nheads, dv = q.shape
    npages = page_table.shape[1]
    rows = cache_k.shape[1]
    nsteps = npages // PAGES_PER_STEP

    def page_spec(r):
        return pl.BlockSpec((1, rows, dv), lambda b, p, pt: (pt[b, p * PAGES_PER_STEP + r], 0, 0))

    seq_spec = pl.BlockSpec((1, nheads, dv), lambda b, p, pt: (b, 0, 0))
    return pl.pallas_call(
        functools.partial(_dec_diff_kernel, lam_init=lam_init, nheads=nheads),
        grid_spec=pltpu.PrefetchScalarGridSpec(
            num_scalar_prefetch=1,
            grid=(nseq, nsteps),
            in_specs=[seq_spec, seq_spec, seq_spec,
                      pl.BlockSpec(lam_vecs.shape, lambda b, p, pt: (0, 0)),
                      pl.BlockSpec((1, dv), lambda b, p, pt: (0, 0))]
                     + [page_spec(r) for r in range(PAGES_PER_STEP)] * 2,
            out_specs=seq_spec,
            scratch_shapes=[pltpu.VMEM((2 * nheads, 1), F32), pltpu.VMEM((2 * nheads, 1), F32),
                            pltpu.VMEM((2 * nheads, dv), F32)]),
        out_shape=jax.ShapeDtypeStruct((nseq, nheads, dv), F32),
        compiler_params=_cparams(("arbitrary", "arbitrary")),
    )(page_table, q, k_new, v_new, lam_vecs, g_sub.reshape(1, dv),
      *([cache_k] * PAGES_PER_STEP), *([cache_v] * PAGES_PER_STEP))


def _dec_gate_kernel(pt_ref, q_ref, *refs, nh, pages_per_block, nblocks):
    del pt_ref
    pps = len(refs) - 2
    k_refs = refs[:pps]
    o_ref, gate_ref = refs[pps:]
    p = pl.program_id(1)

    @pl.when(p == 0)
    def _():
        gate_ref[...] = jnp.zeros_like(gate_ref)

    qb = q_ref[0]
    lane = lax.broadcasted_iota(I32, (nh, LANES), 1)
    g = gate_ref[...]
    for r in range(0, pps, pages_per_block):
        tot = k_refs[r][0]
        for e in range(1, pages_per_block):
            tot = tot + k_refs[r + e][0]
        prod = tot * qb
        per_head = jnp.concatenate(
            [jnp.sum(prod[h * HEAD_DIM:(h + 1) * HEAD_DIM], axis=0, keepdims=True) for h in range(nh)], axis=0)
        blk = (p * pps + r) // pages_per_block
        g = g + jnp.where(lane == blk, jnp.sum(per_head, axis=1, keepdims=True), 0.0)
    gate_ref[...] = g

    @pl.when(p == pl.num_programs(1) - 1)
    def _():
        gate = jnp.where(lane < nblocks, g, -jnp.inf)
        out = jnp.zeros((nh, LANES), I32)
        for r in range(MOBA_TOPK):
            m = jnp.max(gate, axis=1, keepdims=True)
            idx = jnp.min(jnp.where(gate == m, lane, LANES), axis=1, keepdims=True)
            out = jnp.where(lane == r, idx, out)
            gate = jnp.where(lane == idx, -jnp.inf, gate)
        o_ref[0] = out


def _dec_gate(page_table, q, cache_kT):
    nseq, w = q.shape
    npages = page_table.shape[1]
    page = cache_kT.shape[2]
    nh = w // HEAD_DIM
    ppb = MOBA_BLOCK // page
    nblocks = npages // ppb
    pps = _pick(npages, (GATE_PAGES_PER_STEP, PAGES_PER_STEP))
    assert nblocks <= LANES and pps % ppb == 0

    def page_spec(r):
        return pl.BlockSpec((1, w, page), lambda b, p, pt: (pt[b, p * pps + r], 0, 0))

    return pl.pallas_call(
        functools.partial(_dec_gate_kernel, nh=nh, pages_per_block=ppb, nblocks=nblocks),
        grid_spec=pltpu.PrefetchScalarGridSpec(
            num_scalar_prefetch=1,
            grid=(nseq, npages // pps),
            in_specs=[pl.BlockSpec((1, w, page), lambda b, p, pt: (b, 0, 0))]
                     + [page_spec(r) for r in range(pps)],
            out_specs=pl.BlockSpec((1, nh, LANES), lambda b, p, pt: (b, 0, 0)),
            scratch_shapes=[pltpu.VMEM((nh, LANES), F32)]),
        out_shape=jax.ShapeDtypeStruct((nseq, nh, LANES), I32),
        compiler_params=_cparams(("arbitrary", "arbitrary")),
    )(page_table, jnp.broadcast_to(q[:, :, None], (nseq, w, page)), *([cache_kT] * pps))


def _dec_moba_kernel(pt_ref, sel_ref, q_ref, kn_ref, vn_ref, *refs, npg):
    del pt_ref, sel_ref
    k_refs = refs[:npg]
    v_refs = refs[npg:2 * npg]
    o_ref = refs[-1]
    q = q_ref[0, 0]
    q8 = jnp.where(lax.broadcasted_iota(I32, (8, HEAD_DIM), 0) == 0, jnp.broadcast_to(q, (8, HEAD_DIM)), 0.0)
    q16 = q8.astype(BF16)
    st = jnp.concatenate([jnp.dot(q16, k_refs[r][0].astype(BF16), preferred_element_type=F32) for r in range(npg)],
                         axis=1)
    s_new = jnp.sum(q8 * kn_ref[0, 0], axis=1, keepdims=True)
    m = jnp.maximum(jnp.max(st, axis=1, keepdims=True), s_new)
    pr = jnp.exp(st - m)
    p_new = jnp.exp(s_new - m)
    l = jnp.sum(pr, axis=1, keepdims=True) + p_new
    pr16 = pr.astype(BF16)
    page = k_refs[0].shape[2]
    acc = p_new * vn_ref[0, 0]
    for r in range(npg):
        acc = acc + lax.dot_general(pr16[:, r * page:(r + 1) * page], v_refs[r][0].astype(BF16), (_NT, ((), ())),
                                    preferred_element_type=F32)
    o_ref[0, 0] = (acc / l)[0:1]


def _dec_moba(page_table, sel, q, k_new, v_new, cache_kT, cache_vT):
    nseq, nh = q.shape[:2]
    page = cache_kT.shape[2]
    ppb = MOBA_BLOCK // page
    npg = MOBA_TOPK * ppb

    def page_spec(r):
        def imap(b, h, pt, sl):
            blk = sl[b, h * MOBA_TOPK + r // ppb]
            return (pt[b, blk * ppb + r % ppb], h, 0)
        return pl.BlockSpec((1, HEAD_DIM, page), imap)

    seq_spec = pl.BlockSpec((1, 1, 1, HEAD_DIM), lambda b, h, pt, sl: (b, h, 0, 0))
    return pl.pallas_call(
        functools.partial(_dec_moba_kernel, npg=npg),
        grid_spec=pltpu.PrefetchScalarGridSpec(
            num_scalar_prefetch=2,
            grid=(nseq, nh),
            in_specs=[seq_spec, seq_spec, seq_spec] + [page_spec(r) for r in range(npg)] * 2,
            out_specs=seq_spec),
        out_shape=jax.ShapeDtypeStruct((nseq, nh, 1, HEAD_DIM), F32),
        compiler_params=_cparams(("arbitrary", "arbitrary")),
    )(page_table, sel, q, k_new, v_new, *([cache_kT] * npg), *([cache_vT] * npg))


def _layer(xp, xs, cp, cs, ckd, cvd, ckm, cvm, page_table, lam_init, w_ada, b_ada, g_attn_pre, g_attn_post,
           g_ffn_pre, g_ffn_post, w_in, w_out, lq1, lk1, lq2, lk2, g_subln, w_router, b_router,
           w_exp_gate, w_exp_up, w_exp_down, w_sh_gate, w_sh_up, w_sh_down):
    b, s, d = xp.shape
    nseq = xs.shape[0]
    past = page_table.shape[1] * ckd.shape[1]
    w = w_in.shape[1] // 6
    assert s % MOBA_BLOCK == 0 and MOBA_BLOCK % ckd.shape[1] == 0
    assert page_table.shape[1] % PAGES_PER_STEP == 0 and past // MOBA_BLOCK >= MOBA_TOPK and past % MOBA_BLOCK == 0
    assert xs.shape[1] == 1 and nseq % LANES == 0 and (b * s) % LANES == 0

    mod = _adaln(jnp.concatenate([cp, cs], axis=0), w_ada, b_ada)
    mod_p = [m[:, None, :] for m in jnp.split(mod[:b], 6, axis=-1)]
    mod_s = [m[None, :, :] for m in jnp.split(mod[b:], 6, axis=-1)]

    w_in16 = w_in.astype(BF16)
    w_out16 = w_out.astype(BF16)
    lam_vecs = jnp.stack([lq1, lk1, lq2, lk2]).astype(F32)
    g_col = g_subln.reshape(-1, 1)

    tabs_p = _rope_tables(jnp.arange(s, dtype=I32))
    kd, vd, km, vm, qm, qdT, qmT, kdh, kmh, vdT, vmT = _qkv(xp, mod_p[1], mod_p[0], g_attn_pre, w_in16, tabs_p, True)
    sel = _moba_select(qm, km)
    od = _flash(qdT, kdh, vdT, (lam_vecs, g_col), True, lam_init)
    om = _flash(qmT, kmh, vmT, (sel,), False, lam_init)
    x1_p, h2_p, sT_p = _post_attn(od, om, xp, mod_p[2], mod_p[4], mod_p[3], g_attn_post, g_ffn_pre, w_out16, w_router.T)

    tabs_s = _rope_tables(jnp.full((nseq,), past, dtype=I32))
    qd_s, kd_s, vd_s, qm_s, km_s, vm_s = _qkv(xs.reshape(1, nseq, d), mod_s[1], mod_s[0], g_attn_pre, w_in16, tabs_s, False)
    n_pool, page, ndh, ddh = ckd.shape
    nmh = ckm.shape[2]
    rows_view = lambda c: c.reshape(n_pool, page * ndh, ddh)
    cols_view = lambda c: jnp.transpose(c, (0, 2, 3, 1)).reshape(n_pool, nmh * HEAD_DIM, page)
    per_head = lambda a: a[0].reshape(nseq, ndh, ddh)
    per_mhead = lambda a: a[0].reshape(nseq, nmh, 1, HEAD_DIM)
    od_s = _dec_diff(page_table, per_head(qd_s), per_head(kd_s), per_head(vd_s), rows_view(ckd), rows_view(cvd),
                     lam_vecs, g_subln, lam_init)
    ckmT = cols_view(ckm)
    sel_s = _dec_gate(page_table, qm_s[0], ckmT)
    om_s = _dec_moba(page_table, sel_s[:, :, :MOBA_TOPK].reshape(nseq, -1), per_mhead(qm_s), per_mhead(km_s),
                     per_mhead(vm_s), ckmT, cols_view(cvm))
    x1_s, h2_s, sT_s = _post_attn(od_s.reshape(1, nseq, w), om_s.reshape(1, nseq, w), xs.reshape(1, nseq, d),
                                  mod_s[2], mod_s[4], mod_s[3], g_attn_post, g_ffn_pre, w_out16, w_router.T)

    n_all = b * s + nseq
    h2 = jnp.concatenate([h2_p.reshape(b * s, d), h2_s.reshape(nseq, d)], axis=0)
    idx, pos, wts, cnt = _route(jnp.concatenate([sT_p, sT_s], axis=1), b_router)
    ne = w_router.shape[1]
    counts = cnt[:, 0].astype(I32)
    padded = (counts + MOE_BLOCK - 1) // MOE_BLOCK * MOE_BLOCK
    pad_end = jnp.cumsum(padded)
    pad_start = pad_end - padded
    n_blk = n_all * TOP_K // MOE_BLOCK + ne
    e_iota = jnp.arange(ne, dtype=I32)
    blk_start = jnp.arange(n_blk, dtype=I32) * MOE_BLOCK
    blk_e = jnp.minimum(jnp.sum((pad_end[None, :] <= blk_start[:, None]).astype(I32), axis=1), ne - 1)
    blk_end = jnp.sum(jnp.where(blk_e[:, None] == e_iota[None, :], (pad_start + counts)[None, :], 0), axis=1)
    blk_valid = jnp.clip(blk_end - blk_start, 0, MOE_BLOCK).astype(I32)
    dest = pos + jnp.sum(jnp.where(idx[None] == e_iota[:, None, None], pad_start[:, None, None], 0), axis=0)
    xs_sorted = _dispatch(dest, h2, n_blk * MOE_BLOCK)
    wgu16 = jnp.concatenate([w_exp_gate, w_exp_up], axis=-1).astype(BF16)
    ys = _experts(blk_e, blk_valid, xs_sorted, wgu16, w_exp_down.astype(BF16))
    wsgu16 = jnp.concatenate([w_sh_gate, w_sh_up], axis=-1).astype(BF16)
    wsd16 = w_sh_down.astype(BF16)
    wts_t = wts.T
    y_p = _combine(dest, ys, wts_t, h2, x1_p, mod_p[5], g_ffn_post, wsgu16, wsd16, 0)
    y_s = _combine(dest, ys, wts_t, h2, x1_s, mod_s[5], g_ffn_post, wsgu16, wsd16, b * s)

    hd = ckd.shape[2:]
    hm = ckm.shape[2:]
    rows_p = (kd.reshape((b, s) + hd), vd.reshape((b, s) + hd), km.reshape((b, s) + hm), vm.reshape((b, s) + hm))
    rows_s = (kd_s.reshape((nseq, 1) + hd), vd_s.reshape((nseq, 1) + hd), km_s.reshape((nseq, 1) + hm), vm_s.reshape((nseq, 1) + hm))
    return y_p, y_s.reshape(nseq, 1, d), rows_p, rows_s


def kernel(x_prompt, x_sample, c_prompt, c_sample, cache_k_diff, cache_v_diff, cache_k_moba, cache_v_moba, page_table, w_ada, b_ada, g_attn_pre, g_attn_post, g_ffn_pre, g_ffn_post, w_in, w_out, lambda_q1, lambda_k1, lambda_q2, lambda_k2, g_subln, w_router, b_router, w_exp_gate, w_exp_up, w_exp_down, w_sh_gate, w_sh_up, w_sh_down):
    import math
    depth = w_ada.shape[0]
    yp, ys = x_prompt, x_sample
    rows_p, rows_s = [], []
    for l in range(depth):
        lam_init = 0.8 - 0.6 * math.exp(-0.3 * l)
        yp, ys, rp, rs = _layer(
            yp, ys, c_prompt, c_sample, cache_k_diff[l], cache_v_diff[l], cache_k_moba[l], cache_v_moba[l], page_table,
            lam_init, w_ada[l], b_ada[l], g_attn_pre[l], g_attn_post[l], g_ffn_pre[l], g_ffn_post[l], w_in[l], w_out[l],
            lambda_q1[l], lambda_k1[l], lambda_q2[l], lambda_k2[l], g_subln[l], w_router[l], b_router[l],
            w_exp_gate[l], w_exp_up[l], w_exp_down[l], w_sh_gate[l], w_sh_up[l], w_sh_down[l])
        rows_p.append(rp)
        rows_s.append(rs)
    stack = lambda rows, i: jnp.stack([r[i] for r in rows])
    return (yp, ys, stack(rows_p, 0), stack(rows_p, 1), stack(rows_p, 2), stack(rows_p, 3),
            stack(rows_s, 0), stack(rows_s, 1), stack(rows_s, 2), stack(rows_s, 3))
```
